```python
import math
import jax, jax.numpy as jnp
from jax import lax
import numpy as np

D_MODEL = 1024
BATCH = 16
SEQ = 256
DEPTH = 4
DEC_BATCH = 4
DEC_SEQ = 1024
PAST_LEN = 512

GRID_W = 64
EPS = 1e-6
SSD_HEADS = 16
SSD_HEAD_DIM = 64
D_INNER = SSD_HEADS * SSD_HEAD_DIM
SSD_GROUPS = 2
HEADS_PER_GROUP = SSD_HEADS // SSD_GROUPS
SSD_STATE = 128
SSD_CONV = 3
SSD_CHUNK = 128
CONV_CH = D_INNER + 2 * SSD_GROUPS * SSD_STATE
MLA_HEADS = 16
Q_LORA = 384
KV_LORA = 256
QK_NOPE = 64
ROPE_DIM = 32
V_DIM = 64
ROPE_THETA = 10000.0
Q_BLOCK = 128
ATTN_SCALE = (QK_NOPE + ROPE_DIM) ** -0.5
FFT_GROUPS = 4
FFT_GROUP_W = 256
FFT_W = FFT_GROUPS * FFT_GROUP_W
D_FF = 2816
FFN_CONV = 3
N_BRANCH = 3
OFF_XBC = D_INNER
OFF_DT = OFF_XBC + CONV_CH
OFF_QD = OFF_DT + 2 * SSD_HEADS
OFF_KVD = OFF_QD + Q_LORA
OFF_FFT = OFF_KVD + KV_LORA + ROPE_DIM
IN_COLS = OFF_FFT + FFT_W

kernel_name = "hybrid_ssd_mla_fnet_diffusion_step"


def rms_norm(x, g):
    xf = x.astype(jnp.float32)
    y = xf * lax.rsqrt(jnp.mean(xf * xf, axis=-1, keepdims=True) + EPS)
    return (y * g.astype(jnp.float32)).astype(x.dtype)


def adaln_params(cvec, w, b):
    m = jax.nn.silu(cvec) @ w + b
    return jnp.split(m[:, None, :], 6, axis=-1)


def dwconv_centred(x, w, b):
    k = w.shape[0]
    y = lax.conv_general_dilated(x, w[:, None, :].astype(x.dtype), (1,), [(k // 2, k // 2)],
                                 dimension_numbers=("NWC", "WIO", "NWC"),
                                 feature_group_count=x.shape[-1])
    return y + b


def rope_2d(x):
    n = x.shape[1]
    rows = n // GRID_W
    row = jnp.repeat(jnp.arange(rows), GRID_W)
    col = jnp.tile(jnp.arange(GRID_W), rows)
    quarter = ROPE_DIM // 4
    inv = ROPE_THETA ** (-jnp.arange(quarter, dtype=jnp.float32) / quarter)
    bshape = (n,) + (1,) * (x.ndim - 3) + (quarter,)

    def rot(xa, pos):
        ang = (pos.astype(jnp.float32)[:, None] * inv).reshape(bshape)
        cos, sin = jnp.cos(ang).astype(x.dtype), jnp.sin(ang).astype(x.dtype)
        x1, x2 = jnp.split(xa, 2, axis=-1)
        return jnp.concatenate([x1 * cos - x2 * sin, x1 * sin + x2 * cos], axis=-1)

    x_row, x_col = jnp.split(x, 2, axis=-1)
    return jnp.concatenate([rot(x_row, row), rot(x_col, col)], axis=-1)


def ssd_scan(x, dt, a, bm, cm, h0):
    b, l, g, e, p = x.shape
    n = bm.shape[-1]
    nc = l // SSD_CHUNK
    f32 = jnp.float32
    xdt = (x.astype(f32) * dt[..., None]).reshape(b, nc, SSD_CHUNK, g, e, p)
    bc = bm.astype(f32).reshape(b, nc, SSD_CHUNK, g, n)
    cc = cm.astype(f32).reshape(b, nc, SSD_CHUNK, g, n)
    a_cum = jnp.cumsum((dt * a).reshape(b, nc, SSD_CHUNK, g, e), axis=2)
    lower = jnp.tril(jnp.ones((SSD_CHUNK, SSD_CHUNK), bool))[:, :, None, None]
    seg = a_cum[:, :, :, None] - a_cum[:, :, None, :]
    decay = jnp.where(lower, jnp.exp(jnp.where(lower, seg, 0.0)), 0.0)
    cb = jnp.einsum("bcign,bcjgn->bcijg", cc, bc)
    y_diag = jnp.einsum("bcijg,bcijge,bcjgep->bcigep", cb, decay, xdt)
    to_end = jnp.exp(a_cum[:, :, -1:] - a_cum)
    chunk_states = jnp.einsum("bcjgn,bcjge,bcjgep->bcgepn", bc, to_end, xdt)
    chunk_decay = jnp.exp(a_cum[:, :, -1])

    def step(h, inp):
        s, d = inp
        return h * d[..., None, None] + s, h

    h_last, h_enter = lax.scan(step, h0.astype(f32),
                               (jnp.moveaxis(chunk_states, 1, 0), jnp.moveaxis(chunk_decay, 1, 0)))
    h_enter = jnp.moveaxis(h_enter, 0, 1)
    y_off = jnp.einsum("bcign,bcige,bcgepn->bcigep", cc, jnp.exp(a_cum), h_enter)
    y = (y_diag + y_off).reshape(b, l, g, e, p)
    return y.astype(x.dtype), h_last.astype(x.dtype)


def flip_seq(t):
    return jnp.flip(t, axis=1)


def ssd_branch(z, xbc, dt_raw, h0, lw):
    b, l, _ = z.shape
    f32 = jnp.float32
    xbc = jax.nn.silu(dwconv_centred(xbc, lw["conv_ssd_w"], lw["conv_ssd_b"]))
    xs = xbc[..., :D_INNER].reshape(b, l, SSD_GROUPS, HEADS_PER_GROUP, SSD_HEAD_DIM)
    bm = xbc[..., D_INNER:D_INNER + SSD_GROUPS * SSD_STATE].reshape(b, l, SSD_GROUPS, SSD_STATE)
    cm = xbc[..., D_INNER + SSD_GROUPS * SSD_STATE:].reshape(b, l, SSD_GROUPS, SSD_STATE)
    dt = jax.nn.softplus(dt_raw.astype(f32).reshape(b, l, 2, SSD_GROUPS, HEADS_PER_GROUP)
                         + lw["dt_bias"].astype(f32).reshape(2, SSD_GROUPS, HEADS_PER_GROUP))
    a = -jnp.exp(lw["a_log"].astype(f32)).reshape(2, SSD_GROUPS, HEADS_PER_GROUP)
    h0 = h0.reshape(b, 2, SSD_GROUPS, HEADS_PER_GROUP, SSD_HEAD_DIM, SSD_STATE)
    y_f, h_f = ssd_scan(xs, dt[:, :, 0], a[0], bm, cm, h0[:, 0])
    y_b, h_b = ssd_scan(flip_seq(xs), flip_seq(dt[:, :, 1]), a[1], flip_seq(bm), flip_seq(cm), h0[:, 1])
    y = y_f + flip_seq(y_b) + lw["d_skip"].reshape(SSD_GROUPS, HEADS_PER_GROUP, 1) * xs
    y = rms_norm(y.reshape(b, l, D_INNER) * jax.nn.silu(z), lw["g_ssd_norm"])
    h_final = jnp.stack([h_f, h_b], axis=1).reshape(b, 2, SSD_HEADS, SSD_HEAD_DIM, SSD_STATE)
    return y @ lw["w_ssd_out"], h_final


def mla_attend(q_nope, q_pe, ckv, kpe, w_kv_up):
    b, lq = q_nope.shape[:2]
    kv = jnp.einsum("bkr,rhd->bkhd", ckv, w_kv_up.reshape(KV_LORA, MLA_HEADS, QK_NOPE + V_DIM))
    k_nope, v = kv[..., :QK_NOPE], kv[..., QK_NOPE:]
    nb = lq // Q_BLOCK

    def to_blocks(t):
        return jnp.moveaxis(t.reshape((b, nb, Q_BLOCK) + t.shape[2:]), 1, 0)

    def one_block(args):
        qn, qp = args
        s = (jnp.einsum("bqhd,bkhd->bhqk", qn, k_nope)
             + jnp.einsum("bqhr,bkr->bhqk", qp, kpe))
        pr = jax.nn.softmax(s.astype(jnp.float32) * ATTN_SCALE, axis=-1).astype(v.dtype)
        return jnp.einsum("bhqk,bkhd->bqhd", pr, v)

    o = lax.map(one_block, (to_blocks(q_nope), to_blocks(q_pe)))
    return jnp.moveaxis(o, 0, 1).reshape(b, lq, MLA_HEADS * V_DIM)


def fourier_branch(f):
    b, l, _ = f.shape
    fg = f.astype(jnp.float32).reshape(b, l, FFT_GROUPS, FFT_GROUP_W)
    mixed = jnp.fft.fft2(fg, axes=(1, 3), norm="ortho").real
    return mixed.reshape(b, l, FFT_W).astype(f.dtype)


def token_mixing(h, lw, ctx):
    b, l, _ = h.shape
    proj = h @ lw["w_in"]
    z = proj[..., :OFF_XBC]
    xbc = proj[..., OFF_XBC:OFF_DT]
    dt_raw = proj[..., OFF_DT:OFF_QD]
    q_down = proj[..., OFF_QD:OFF_KVD]
    kv_down = proj[..., OFF_KVD:OFF_FFT]
    f_in = proj[..., OFF_FFT:]
    q = (rms_norm(q_down, lw["g_q_norm"]) @ lw["w_q_up"]).reshape(b, l, MLA_HEADS, QK_NOPE + ROPE_DIM)
    q_nope, q_pe = q[..., :QK_NOPE], q[..., QK_NOPE:]
    ckv = rms_norm(kv_down[..., :KV_LORA], lw["g_kv_norm"])
    kpe = kv_down[..., KV_LORA:]
    if ctx is None:
        h0 = jnp.zeros((b, 2, SSD_HEADS, SSD_HEAD_DIM, SSD_STATE), h.dtype)
        keys_ckv, keys_kpe = ckv, kpe
    else:
        ckv_ctx, kpe_ctx, h0 = ctx
        q_pe = rope_2d(q_pe)
        keys_ckv = jnp.concatenate([ckv, ckv_ctx.astype(ckv.dtype)], axis=1)
        keys_kpe = jnp.concatenate([rope_2d(kpe), kpe_ctx.astype(kpe.dtype)], axis=1)
    o_mla = mla_attend(q_nope, q_pe, keys_ckv, keys_kpe, lw["w_kv_up"]) @ lw["w_mla_out"]
    o_ssd, h_final = ssd_branch(z, xbc, dt_raw, h0, lw)
    o_fft = fourier_branch(f_in) @ lw["w_fft_out"]
    gates = jax.nn.sigmoid(h @ lw["w_gate"] + lw["b_gate"])
    g_ssd, g_mla, g_fft = jnp.split(gates, N_BRANCH, axis=-1)
    out = (g_ssd * o_ssd + g_mla * o_mla + g_fft * o_fft) @ lw["w_o"]
    return out, (ckv, kpe, h_final)


def conv_mlp(h, lw):
    u = dwconv_centred(h @ lw["w_up"], lw["conv_ffn_w"], lw["conv_ffn_b"])
    a, g = jnp.split(u, 2, axis=-1)
    return (jax.nn.silu(a) * g) @ lw["w_down"]


def trunk_layer(x, cvec, lw, ctx):
    sh1, sc1, g1, sh2, sc2, g2 = adaln_params(cvec, lw["w_ada"], lw["b_ada"])
    h = rms_norm(x, lw["g_pre_mix"]) * (1 + sc1) + sh1
    mix, ctx_out = token_mixing(h, lw, ctx)
    x = x + g1 * rms_norm(mix, lw["g_post_mix"])
    h = rms_norm(x, lw["g_pre_ffn"]) * (1 + sc2) + sh2
    x = x + g2 * rms_norm(conv_mlp(h, lw), lw["g_post_ffn"])
    return x, ctx_out


def setup_inputs(seed: int = 0) -> dict:
    key = jax.random.key(seed)
    keys = iter(jax.random.split(key, 64))
    L, D = DEPTH, D_MODEL

    def normal(shape, scale):
        return jax.random.normal(next(keys), shape, jnp.float32) * scale

    def gain(shape):
        return 1.0 + normal(shape, 0.05)

    dt0 = jnp.exp(jax.random.uniform(next(keys), (L, 2, SSD_HEADS), jnp.float32,
                                     math.log(1e-3), math.log(1e-1)))
    dt_bias = dt0 + jnp.log(-jnp.expm1(-dt0))
    a_log = jnp.log(jax.random.uniform(next(keys), (L, 2, SSD_HEADS), jnp.float32, 1.0, 16.0))
    return {
        "x_prompt": normal((BATCH, SEQ, D), 1.0),
        "x_sample": normal((DEC_BATCH, DEC_SEQ, D), 1.0),
        "cache_ckv": normal((DEC_BATCH, L, PAST_LEN, KV_LORA), 1.0),
        "cache_kpe": normal((DEC_BATCH, L, PAST_LEN, ROPE_DIM), 1.0),
        "state_ssm": normal((DEC_BATCH, L, 2, SSD_HEADS, SSD_HEAD_DIM, SSD_STATE), 0.3),
        "c": normal((DEC_BATCH, D), 1.0),
        "c_ctx": normal((D,), 1.0),
        "w_ada": normal((L, D, 6 * D), 0.5 * D ** -0.5),
        "b_ada": normal((L, 6 * D), 0.01),
        "g_pre_mix": gain((L, D)),
        "g_post_mix": gain((L, D)),
        "g_pre_ffn": gain((L, D)),
        "g_post_ffn": gain((L, D)),
        "w_in": normal((L, D, IN_COLS), D ** -0.5),
        "w_gate": normal((L, D, N_BRANCH * D), D ** -0.5),
        "b_gate": normal((L, N_BRANCH * D), 0.01),
        "w_o": normal((L, D, D), D ** -0.5),
        "conv_ssd_w": normal((L, SSD_CONV, CONV_CH), SSD_CONV ** -0.5),
        "conv_ssd_b": normal((L, CONV_CH), 0.01),
        "dt_bias": dt_bias,
        "a_log": a_log,
        "d_skip": gain((L, SSD_HEADS)),
        "g_ssd_norm": gain((L, D_INNER)),
        "w_ssd_out": normal((L, D_INNER, D), D_INNER ** -0.5),
        "g_q_norm": gain((L, Q_LORA)),
        "w_q_up": normal((L, Q_LORA, MLA_HEADS * (QK_NOPE + ROPE_DIM)), Q_LORA ** -0.5),
        "g_kv_norm": gain((L, KV_LORA)),
        "w_kv_up": normal((L, KV_LORA, MLA_HEADS * (QK_NOPE + V_DIM)), KV_LORA ** -0.5),
        "w_mla_out": normal((L, MLA_HEADS * V_DIM, D), (MLA_HEADS * V_DIM) ** -0.5),
        "w_fft_out": normal((L, FFT_W, D), FFT_W ** -0.5),
        "w_up": normal((L, D, 2 * D_FF), D ** -0.5),
        "conv_ffn_w": normal((L, FFN_CONV, 2 * D_FF), FFN_CONV ** -0.5),
        "conv_ffn_b": normal((L, 2 * D_FF), 0.01),
        "w_down": normal((L, D_FF, D), D_FF ** -0.5),
    }


def reference(x_prompt, x_sample, cache_ckv, cache_kpe, state_ssm, c, c_ctx,
              w_ada, b_ada, g_pre_mix, g_post_mix, g_pre_ffn, g_post_ffn,
              w_in, w_gate, b_gate, w_o, conv_ssd_w, conv_ssd_b, dt_bias, a_log, d_skip,
              g_ssd_norm, w_ssd_out, g_q_norm, w_q_up, g_kv_norm, w_kv_up, w_mla_out,
              w_fft_out, w_up, conv_ffn_w, conv_ffn_b, w_down):
    params = {
        "w_ada": w_ada, "b_ada": b_ada, "g_pre_mix": g_pre_mix, "g_post_mix": g_post_mix,
        "g_pre_ffn": g_pre_ffn, "g_post_ffn": g_post_ffn, "w_in": w_in, "w_gate": w_gate,
        "b_gate": b_gate, "w_o": w_o, "conv_ssd_w": conv_ssd_w, "conv_ssd_b": conv_ssd_b,
        "dt_bias": dt_bias, "a_log": a_log, "d_skip": d_skip, "g_ssd_norm": g_ssd_norm,
        "w_ssd_out": w_ssd_out, "g_q_norm": g_q_norm, "w_q_up": w_q_up, "g_kv_norm": g_kv_norm,
        "w_kv_up": w_kv_up, "w_mla_out": w_mla_out, "w_fft_out": w_fft_out, "w_up": w_up,
        "conv_ffn_w": conv_ffn_w, "conv_ffn_b": conv_ffn_b, "w_down": w_down,
    }
    y_prompt, y_sample = x_prompt, x_sample
    ckv_list, kpe_list, ssm_list = [], [], []
    c_context = c_ctx[None, :]
    for i in range(DEPTH):
        lw = {name: arr[i] for name, arr in params.items()}
        y_prompt, (ckv, kpe, ssm) = trunk_layer(y_prompt, c_context, lw, None)
        ckv_list.append(ckv)
        kpe_list.append(kpe)
        ssm_list.append(ssm)
        y_sample, _ = trunk_layer(y_sample, c, lw,
                                  (cache_ckv[:, i], cache_kpe[:, i], state_ssm[:, i]))
    new_ckv = jnp.stack(ckv_list, axis=1)
    new_kpe = jnp.stack(kpe_list, axis=1)
    new_ssm = jnp.stack(ssm_list, axis=1)
    return (y_prompt, y_sample, new_ckv, new_kpe, new_ssm)
```

```python
import functools
import math

import numpy as np
import jax
import jax.numpy as jnp
from jax import lax
from jax.experimental import pallas as pl
from jax.experimental.pallas import tpu as pltpu

F32 = jnp.float32
BF16 = jnp.bfloat16

D_MODEL = 1024
DEPTH = 4
GRID_W = 64
EPS = 1e-6
SSD_HEADS = 16
SSD_HEAD_DIM = 64
D_INNER = SSD_HEADS * SSD_HEAD_DIM
SSD_GROUPS = 2
SSD_STATE = 128
SSD_CHUNK = 128
CONV_CH = D_INNER + 2 * SSD_GROUPS * SSD_STATE
MLA_HEADS = 16
Q_LORA = 384
KV_LORA = 256
QK_NOPE = 64
ROPE_DIM = 32
V_DIM = 64
ROPE_THETA = 10000.0
ATTN_SCALE = (QK_NOPE + ROPE_DIM) ** -0.5
FFT_GROUPS = 4
FFT_GROUP_W = 256
D_FF = 2816
N_BRANCH = 3
OFF_XBC = D_INNER
OFF_DT = OFF_XBC + CONV_CH
OFF_QD = OFF_DT + 2 * SSD_HEADS
OFF_KVD = OFF_QD + Q_LORA
OFF_FFT = OFF_KVD + KV_LORA + ROPE_DIM

LANES = 128
HEAD_SLOT = LANES
KV_PAD = KV_LORA + LANES
DT_PAD = LANES
SEG_Z = 0
SEG_XBC = SEG_Z + D_INNER
SEG_DT = SEG_XBC + CONV_CH
SEG_QD = SEG_DT + DT_PAD
SEG_KVD = SEG_QD + Q_LORA
SEG_FFT = SEG_KVD + KV_PAD
SEG_GATE = SEG_FFT + D_MODEL
PACKED_COLS = SEG_GATE + N_BRANCH * D_MODEL

ROW_TILE = 256
FFN_ROWS = 1024
FFN_BLOCK = 256
Q_BLOCK_ROWS = 256
VMEM_LIMIT = 56 * 1024 * 1024
NEG_BIG = -1e30


def _params(sem, vmem=VMEM_LIMIT):
    return pltpu.CompilerParams(dimension_semantics=sem, vmem_limit_bytes=vmem)


def _sigmoid(x):
    return 1.0 / (1.0 + jnp.exp(-x))


def _silu(x):
    return x * _sigmoid(x)


def _rms(x, g):
    return x * lax.rsqrt(jnp.mean(x * x, axis=-1, keepdims=True) + EPS) * g


def _dot(a, b):
    return jnp.dot(a, b, preferred_element_type=F32)


def _dot_nt(a, b):
    return lax.dot_general(a, b, (((1,), (1,)), ((), ())), preferred_element_type=F32)


def _const_spec(shape):
    nd = len(shape)
    return pl.BlockSpec(shape, lambda *_: (0,) * nd, pipeline_mode=pl.Buffered(1))


def _ada_kernel(c_ref, w_ref, b_ref, o_ref):
    c = c_ref[...]
    s = _silu(c).astype(BF16)
    o_ref[0] = _dot(s, w_ref[0].astype(BF16)) + b_ref[0]


def _ada_params(cvec, w_ada, b_ada):
    n_chunk = 6
    return pl.pallas_call(
        _ada_kernel,
        grid=(DEPTH, n_chunk),
        in_specs=[
            pl.BlockSpec((8, D_MODEL), lambda l, j: (0, 0)),
            pl.BlockSpec((1, D_MODEL, D_MODEL), lambda l, j: (l, 0, j)),
            pl.BlockSpec((1, 1, D_MODEL), lambda l, j: (l, 0, j)),
        ],
        out_specs=pl.BlockSpec((1, 8, D_MODEL), lambda l, j: (l, 0, j)),
        out_shape=jax.ShapeDtypeStruct((DEPTH, 8, n_chunk * D_MODEL), F32),
        compiler_params=_params(("parallel", "parallel")),
        name="ada_params",
    )(cvec, w_ada, b_ada.reshape(DEPTH, 1, n_chunk * D_MODEL))


def _pre_kernel(x_ref, sc_ref, sh_ref, g_ref, w_ref, bg_ref,
                z_ref, xbc_ref, dt_ref, qd_ref, kvd_ref, fin_ref, gate_ref):
    x = x_ref[...]
    h = (_rms(x, g_ref[...]) * (1.0 + sc_ref[0]) + sh_ref[0]).astype(BF16)

    def seg(out_ref, start, width, bias_ref=None):
        step = 512
        for c0 in range(0, width, step):
            w = min(step, width - c0)
            r = _dot(h, w_ref[:, start + c0:start + c0 + w])
            if bias_ref is not None:
                r = _sigmoid(r + bias_ref[:, c0:c0 + w])
            out_ref[:, c0:c0 + w] = r.astype(out_ref.dtype)

    seg(z_ref, SEG_Z, D_INNER)
    seg(xbc_ref, SEG_XBC, CONV_CH)
    seg(dt_ref, SEG_DT, DT_PAD)
    seg(qd_ref, SEG_QD, Q_LORA)
    seg(kvd_ref, SEG_KVD, KV_PAD)
    seg(fin_ref, SEG_FFT, D_MODEL)
    seg(gate_ref, SEG_GATE, N_BRANCH * D_MODEL, bg_ref)


def _pre_mix(x, sc, sh, g, w_packed, b_gate, seq):
    t = x.shape[0]
    nb = sc.shape[0]
    rows_per_batch = seq // ROW_TILE

    def mod_map(i):
        return ((i // rows_per_batch) if nb > 1 else 0, 0, 0)

    def row_spec(width):
        return pl.BlockSpec((ROW_TILE, width), lambda i: (i, 0))

    widths = (D_INNER, CONV_CH, DT_PAD, Q_LORA, KV_PAD, D_MODEL, N_BRANCH * D_MODEL)
    dtypes = (F32, F32, F32, F32, F32, BF16, F32)
    return pl.pallas_call(
        _pre_kernel,
        grid=(t // ROW_TILE,),
        in_specs=[
            row_spec(D_MODEL),
            pl.BlockSpec((1, 1, D_MODEL), mod_map),
            pl.BlockSpec((1, 1, D_MODEL), mod_map),
            _const_spec((1, D_MODEL)),
            _const_spec((D_MODEL, PACKED_COLS)),
            _const_spec((1, N_BRANCH * D_MODEL)),
        ],
        out_specs=[row_spec(w) for w in widths],
        out_shape=[jax.ShapeDtypeStruct((t, w), d) for w, d in zip(widths, dtypes)],
        compiler_params=_params(("parallel",)),
        name="pre_mix",
    )(x, sc, sh, g, w_packed, b_gate)


def _split3(x):
    hi = x.astype(BF16)
    r1 = x - hi.astype(F32)
    mid = r1.astype(BF16)
    lo = (r1 - mid.astype(F32)).astype(BF16)
    return hi, mid, lo


def _ssd_kernel(*refs, seq, has_h0, want_state):
    it = iter(refs)
    z_ref, xbc_ref, dt_ref = next(it), next(it), next(it)
    h0_ref = next(it) if has_h0 else None
    cw_ref, cb_ref, dtb_ref, alog_ref, dsk_ref, gn_ref = (next(it) for _ in range(6))
    y_ref = next(it)
    hf_ref = next(it) if want_state else None
    xc_s, xs_s, y_s, dt_s, da_s, h_s = (next(it) for _ in range(6))

    nc = seq // SSD_CHUNK
    rb = SSD_CHUNK

    cstep = 512
    for r0 in range(0, seq, rb):
        for c0 in range(0, CONV_CH, cstep):
            cs = slice(c0, c0 + cstep)
            cur = xbc_ref[r0:r0 + rb, cs]
            rid = lax.broadcasted_iota(jnp.int32, (rb, cstep), 0)
            prev = pltpu.roll(cur, 1, 0)
            if r0 > 0:
                prev = jnp.where(rid == 0, xbc_ref[r0 - 1:r0, cs], prev)
            else:
                prev = jnp.where(rid == 0, 0.0, prev)
            nxt = pltpu.roll(cur, rb - 1, 0)
            if r0 + rb < seq:
                nxt = jnp.where(rid == rb - 1, xbc_ref[r0 + rb:r0 + rb + 1, cs], nxt)
            else:
                nxt = jnp.where(rid == rb - 1, 0.0, nxt)
            u = prev * cw_ref[0:1, cs] + cur * cw_ref[1:2, cs] + nxt * cw_ref[2:3, cs] + cb_ref[:, cs]
            u = _silu(u)
            xc_s[r0:r0 + rb, cs] = u
            if c0 < D_INNER:
                xs_s[r0:r0 + rb, cs] = u.astype(BF16)
                y_s[r0:r0 + rb, cs] = u * dsk_ref[:, cs]

    a_row = -jnp.exp(alog_ref[...])
    for r0 in range(0, seq, rb):
        v = dt_ref[r0:r0 + rb, :] + dtb_ref[...]
        dtv = jnp.maximum(v, 0.0) + jnp.log1p(jnp.exp(-jnp.abs(v)))
        dt_s[r0:r0 + rb, :] = dtv
        da_s[r0:r0 + rb, :] = dtv * a_row

    for d in range(2):
        if has_h0:
            for hp in range(SSD_HEADS // 2):
                blk = h0_ref[0, d, hp * LANES:(hp + 1) * LANES, :]
                h_s[d, :, hp * LANES:(hp + 1) * LANES] = blk.T
        else:
            h_s[d] = jnp.zeros((SSD_STATE, D_INNER), F32)

    ri = lax.broadcasted_iota(jnp.int32, (rb, rb), 0)
    ci = lax.broadcasted_iota(jnp.int32, (rb, rb), 1)
    lower = ri >= ci
    upper = ri <= ci
    tri_f = jnp.where(lower, 1.0, 0.0).astype(BF16)
    tri_b = jnp.where(upper, 1.0, 0.0).astype(BF16)
    first_half = ci < SSD_HEAD_DIM

    def chunk_body(step, carry):
        for d in range(2):
            c = step if d == 0 else nc - 1 - step
            r = pl.multiple_of(c * rb, rb)
            mask = lower if d == 0 else upper
            tri = tri_f if d == 0 else tri_b
            end = rb - 1 if d == 0 else 0
            dtc = dt_s[pl.ds(r, rb), :]
            dac = da_s[pl.ds(r, rb), :]
            hi, mid, lo = _split3(dac)
            acum = _dot(tri, hi) + _dot(tri, mid) + _dot(tri, lo)
            acum_t = acum.T
            dt_t = dtc.T
            w_t = dt_t * jnp.exp(acum_t[:, end:end + 1] - acum_t)
            for g in range(SSD_GROUPS):
                b_g = xc_s[pl.ds(r, rb), D_INNER + g * SSD_STATE:D_INNER + (g + 1) * SSD_STATE]
                c_g = xc_s[pl.ds(r, rb),
                           D_INNER + (SSD_GROUPS + g) * SSD_STATE:D_INNER + (SSD_GROUPS + g + 1) * SSD_STATE]
                cb = _dot_nt(c_g.astype(BF16), b_g.astype(BF16))
                b_t = b_g.T
                for hp in range(g * 4, g * 4 + 4):
                    ls = slice(hp * LANES, (hp + 1) * LANES)
                    x_pair = xs_s[pl.ds(r, rb), ls]
                    h_pair = h_s[d, :, ls]
                    rhs = jnp.concatenate([x_pair, h_pair.astype(BF16)], axis=0)
                    ys, ss, cds = [], [], []
                    for e in (2 * hp, 2 * hp + 1):
                        k = d * SSD_HEADS + e
                        colb = jnp.broadcast_to(acum[:, k:k + 1], (rb, rb))
                        seg = jnp.where(mask, colb - acum_t[k:k + 1, :], NEG_BIG)
                        m_e = jnp.exp(seg) * cb * dt_t[k:k + 1, :]
                        e_col = jnp.exp(colb)
                        cw = c_g * e_col
                        lhs = jnp.concatenate([m_e.astype(BF16), cw.astype(BF16)], axis=1)
                        ys.append(_dot(lhs, rhs))
                        bw_t = (b_t * w_t[k:k + 1, :]).astype(BF16)
                        ss.append(_dot(bw_t, x_pair))
                        cds.append(e_col[end:end + 1, :])
                    y_pair = jnp.where(first_half, ys[0], ys[1])
                    y_s[pl.ds(r, rb), ls] = y_s[pl.ds(r, rb), ls] + y_pair
                    cd = jnp.where(first_half[0:1, :], cds[0], cds[1])
                    h_s[d, :, ls] = h_pair * cd + jnp.where(first_half, ss[0], ss[1])
        return carry

    lax.fori_loop(0, nc, chunk_body, 0)

    for r0 in range(0, seq, rb):
        y = y_s[r0:r0 + rb, :] * _silu(z_ref[r0:r0 + rb, :])
        y_ref[r0:r0 + rb, :] = _rms(y, gn_ref[...]).astype(y_ref.dtype)

    if want_state:
        for d in range(2):
            for hp in range(SSD_HEADS // 2):
                hf_ref[0, d, hp * LANES:(hp + 1) * LANES, :] = h_s[d, :, hp * LANES:(hp + 1) * LANES].T


def _ssd_branch(z, xbc, dt, h0, lw, batch, seq, want_state):
    has_h0 = h0 is not None
    row = lambda w: pl.BlockSpec((seq, w), lambda b: (b, 0))
    state_spec = pl.BlockSpec((1, 2, D_INNER, SSD_STATE), lambda b: (b, 0, 0, 0))
    in_specs = [row(D_INNER), row(CONV_CH), row(DT_PAD)]
    args = [z, xbc, dt]
    if has_h0:
        in_specs.append(state_spec)
        args.append(h0)
    consts = [lw["conv_ssd_w"], lw["conv_ssd_b"], lw["dt_bias"], lw["a_log"], lw["d_skip"], lw["g_ssd_norm"]]
    in_specs += [_const_spec(c.shape) for c in consts]
    args += consts
    out_specs = [row(D_INNER)]
    out_shape = [jax.ShapeDtypeStruct((batch * seq, D_INNER), BF16)]
    if want_state:
        out_specs.append(state_spec)
        out_shape.append(jax.ShapeDtypeStruct((batch, 2, D_INNER, SSD_STATE), F32))
    res = pl.pallas_call(
        functools.partial(_ssd_kernel, seq=seq, has_h0=has_h0, want_state=want_state),
        grid=(batch,),
        in_specs=in_specs,
        out_specs=out_specs,
        out_shape=out_shape,
        scratch_shapes=[
            pltpu.VMEM((seq, CONV_CH), F32),
            pltpu.VMEM((seq, D_INNER), BF16),
            pltpu.VMEM((seq, D_INNER), F32),
            pltpu.VMEM((seq, DT_PAD), F32),
            pltpu.VMEM((seq, DT_PAD), F32),
            pltpu.VMEM((2, SSD_STATE, D_INNER), F32),
        ],
        compiler_params=_params(("parallel",)),
        name="ssd_branch",
    )(*args)
    return (res[0], res[1]) if want_state else (res[0], None)


def _rope(x, cos, sin_next, sin_prev):
    return x * cos + pltpu.roll(x, LANES - 8, 1) * sin_next + pltpu.roll(x, 8, 1) * sin_prev


def _mla_kernel(*refs, seq, n_cache, emit_kv):
    it = iter(refs)
    qd_ref, kvd_ref = next(it), next(it)
    latent = n_cache > 0
    if latent:
        cckv_ref, ckpe_ref = next(it), next(it)
        qcos_ref, qsn_ref, qsp_ref, kcos_ref, ksn_ref, ksp_ref = (next(it) for _ in range(6))
    gq_ref, gkv_ref, wq_ref, wk_ref, wv_ref = (next(it) for _ in range(5))
    o_ref = next(it)
    if emit_kv:
        ckv_out_ref, kpe_out_ref = next(it), next(it)
    q_s, k_s, v_s, o_s = (next(it) for _ in range(4))

    n_keys = seq + n_cache
    rb = 256

    for r0 in range(0, seq, rb):
        qn = _rms(qd_ref[r0:r0 + rb, :], gq_ref[...]).astype(BF16)
        for h in range(MLA_HEADS):
            qh = _dot(qn, wq_ref[:, h * HEAD_SLOT:(h + 1) * HEAD_SLOT])
            if latent:
                qh = _rope(qh, qcos_ref[r0:r0 + rb, :], qsn_ref[r0:r0 + rb, :], qsp_ref[r0:r0 + rb, :])
            q_s[h, r0:r0 + rb, :] = (qh * ATTN_SCALE).astype(BF16)

    def put_keys(row0, ckv_n, kpe_tile):
        kin = jnp.concatenate([ckv_n.astype(BF16), kpe_tile.astype(BF16)], axis=1)
        for h in range(MLA_HEADS):
            kh = _dot(kin, wk_ref[:, h * HEAD_SLOT:(h + 1) * HEAD_SLOT])
            k_s[h, row0:row0 + rb, :] = kh.astype(BF16)
        cb16 = ckv_n.astype(BF16)
        for hp in range(MLA_HEADS // 2):
            vh = _dot(cb16, wv_ref[:, hp * LANES:(hp + 1) * LANES])
            v_s[hp, row0:row0 + rb, :] = vh.astype(BF16)

    for r0 in range(0, seq, rb):
        ckv_n = _rms(kvd_ref[r0:r0 + rb, 0:KV_LORA], gkv_ref[...])
        kpe = kvd_ref[r0:r0 + rb, KV_LORA:KV_PAD]
        if emit_kv:
            ckv_out_ref[r0:r0 + rb, :] = ckv_n
            kpe_out_ref[r0:r0 + rb, :] = kpe
        if latent:
            kpe = _rope(kpe, kcos_ref[r0:r0 + rb, :], ksn_ref[r0:r0 + rb, :], ksp_ref[r0:r0 + rb, :])
        put_keys(r0, ckv_n, kpe)
    for r0 in range(0, n_cache, rb):
        put_keys(seq + r0, cckv_ref[0, r0:r0 + rb, :], ckpe_ref[0, r0:r0 + rb, :])

    nqb = seq // Q_BLOCK_ROWS
    lane = lax.broadcasted_iota(jnp.int32, (Q_BLOCK_ROWS, LANES), 1)
    first_half = lane < V_DIM

    def attend(idx, carry):
        hp = idx // nqb
        qb = idx % nqb
        r = pl.multiple_of(qb * Q_BLOCK_ROWS, Q_BLOCK_ROWS)
        v_pair = v_s[hp]
        outs = []
        for j in range(2):
            h = 2 * hp + j
            s = _dot_nt(q_s[h, pl.ds(r, Q_BLOCK_ROWS), :], k_s[h])
            m = jnp.max(s, axis=-1, keepdims=True)
            p = jnp.exp(s - m)
            l = jnp.sum(p, axis=-1, keepdims=True)
            outs.append(_dot(p.astype(BF16), v_pair) / l)
        o_s[hp, pl.ds(r, Q_BLOCK_ROWS), :] = jnp.where(first_half, outs[0], outs[1]).astype(BF16)
        return carry

    lax.fori_loop(0, (MLA_HEADS // 2) * nqb, attend, 0)

    for hp in range(MLA_HEADS // 2):
        o_ref[:, hp * LANES:(hp + 1) * LANES] = o_s[hp]
    del n_keys


def _mla_branch(qd, kvd, cache, rope_tabs, lw, batch, seq):
    latent = cache is not None
    n_cache = cache[0].shape[1] if latent else 0
    emit_kv = not latent
    row = lambda w: pl.BlockSpec((seq, w), lambda b: (b, 0))
    in_specs = [row(Q_LORA), row(KV_PAD)]
    args = [qd, kvd]
    if latent:
        in_specs += [pl.BlockSpec((1, n_cache, KV_LORA), lambda b: (b, 0, 0)),
                     pl.BlockSpec((1, n_cache, LANES), lambda b: (b, 0, 0))]
        args += list(cache)
        in_specs += [_const_spec((seq, LANES))] * 6
        args += list(rope_tabs)
    consts = [lw["g_q_norm"], lw["g_kv_norm"], lw["w_q_pad"], lw["w_k_pad"], lw["w_v"]]
    in_specs += [_const_spec(c.shape) for c in consts]
    args += consts
    out_specs = [row(D_MODEL)]
    out_shape = [jax.ShapeDtypeStruct((batch * seq, D_MODEL), BF16)]
    if emit_kv:
        out_specs += [row(KV_LORA), row(LANES)]
        out_shape += [jax.ShapeDtypeStruct((batch * seq, KV_LORA), F32),
                      jax.ShapeDtypeStruct((batch * seq, LANES), F32)]
    n_keys = seq + n_cache
    return pl.pallas_call(
        functools.partial(_mla_kernel, seq=seq, n_cache=n_cache, emit_kv=emit_kv),
        grid=(batch,),
        in_specs=in_specs,
        out_specs=out_specs,
        out_shape=out_shape,
        scratch_shapes=[
            pltpu.VMEM((MLA_HEADS, seq, HEAD_SLOT), BF16),
            pltpu.VMEM((MLA_HEADS, n_keys, HEAD_SLOT), BF16),
            pltpu.VMEM((MLA_HEADS // 2, n_keys, LANES), BF16),
            pltpu.VMEM((MLA_HEADS // 2, seq, LANES), BF16),
        ],
        compiler_params=_params(("parallel",)),
        name="mla_branch",
    )(*args)


def _fft_kernel(f_ref, cs_ref, dl_ref, o_ref, *, seq):
    scale = 1.0 / math.sqrt(seq * FFT_GROUP_W)
    for g in range(FFT_GROUPS):
        gs = slice(g * FFT_GROUP_W, (g + 1) * FFT_GROUP_W)
        t = _dot(f_ref[:, gs], cs_ref[...]).astype(BF16)
        stack = jnp.concatenate([t[:, :FFT_GROUP_W], t[:, FFT_GROUP_W:]], axis=0)
        o_ref[:, gs] = (_dot(dl_ref[...], stack) * scale).astype(o_ref.dtype)


def _fft_branch(f_in, cs, dl, batch, seq):
    row = pl.BlockSpec((seq, D_MODEL), lambda b: (b, 0))
    return pl.pallas_call(
        functools.partial(_fft_kernel, seq=seq),
        grid=(batch,),
        in_specs=[row, _const_spec(cs.shape), _const_spec(dl.shape)],
        out_specs=row,
        out_shape=jax.ShapeDtypeStruct((batch * seq, D_MODEL), BF16),
        compiler_params=_params(("parallel",)),
        name="fft_branch",
    )(f_in, cs, dl)


def _dft_tables(seq):
    def cos_sin(n):
        idx = np.arange(n, dtype=np.int64)
        ang = 2.0 * np.pi * ((idx[:, None] * idx[None, :]) % n) / n
        return np.cos(ang), np.sin(ang)

    cc, sc = cos_sin(FFT_GROUP_W)
    cl, sl = cos_sin(seq)
    cs = np.concatenate([cc, sc], axis=1).astype(np.float32)
    dl = np.concatenate([cl, -sl], axis=1).astype(np.float32)
    return jnp.asarray(cs).astype(BF16), jnp.asarray(dl).astype(BF16)


def _merge_kernel(y_ref, o_ref, f_ref, gate_ref, x_ref, g1_ref, gp_ref,
                  ws_ref, wm_ref, wf_ref, wo_ref, out_ref):
    mix = (gate_ref[:, 0:D_MODEL] * _dot(y_ref[...], ws_ref[...])
           + gate_ref[:, D_MODEL:2 * D_MODEL] * _dot(o_ref[...], wm_ref[...])
           + gate_ref[:, 2 * D_MODEL:3 * D_MODEL] * _dot(f_ref[...], wf_ref[...]))
    out = _dot(mix.astype(BF16), wo_ref[...])
    out_ref[...] = x_ref[...] + g1_ref[0] * _rms(out, gp_ref[...])


def _merge(y, o, f, gates, x, g1, gpost, lw, seq):
    t = x.shape[0]
    nb = g1.shape[0]
    rows_per_batch = seq // ROW_TILE

    def mod_map(i):
        return ((i // rows_per_batch) if nb > 1 else 0, 0, 0)

    row = lambda w: pl.BlockSpec((ROW_TILE, w), lambda i: (i, 0))
    wspec = _const_spec((D_MODEL, D_MODEL))
    return pl.pallas_call(
        _merge_kernel,
        grid=(t // ROW_TILE,),
        in_specs=[row(D_MODEL), row(D_MODEL), row(D_MODEL), row(N_BRANCH * D_MODEL), row(D_MODEL),
                  pl.BlockSpec((1, 1, D_MODEL), mod_map), _const_spec((1, D_MODEL)),
                  wspec, wspec, wspec, wspec],
        out_specs=row(D_MODEL),
        out_shape=jax.ShapeDtypeStruct((t, D_MODEL), F32),
        compiler_params=_params(("parallel",)),
        name="merge",
    )(y, o, f, gates, x, g1, gpost, lw["w_ssd_out"], lw["w_mla_out"], lw["w_fft_out"], lw["w_o"])


def _ffn_kernel(x_ref, sc_ref, sh_ref, g2_ref, gpre_ref, gpost_ref,
                wa_ref, wg_ref, cwa_ref, cwg_ref, cba_ref, cbg_ref, wd_ref,
                out_ref, h_s, acc_s, *, seq):
    j = pl.program_id(1)

    @pl.when(j == 0)
    def _():
        h_s[...] = (_rms(x_ref[...], gpre_ref[...]) * (1.0 + sc_ref[0]) + sh_ref[0]).astype(BF16)
        acc_s[...] = jnp.zeros_like(acc_s)

    rid = lax.broadcasted_iota(jnp.int32, (FFN_ROWS, FFN_BLOCK), 0) & (seq - 1)
    is_first = rid == 0
    is_last = rid == seq - 1

    def conv(u, cw_ref, cb_ref):
        prev = jnp.where(is_first, 0.0, pltpu.roll(u, 1, 0))
        nxt = jnp.where(is_last, 0.0, pltpu.roll(u, FFN_ROWS - 1, 0))
        return prev * cw_ref[0:1, :] + u * cw_ref[1:2, :] + nxt * cw_ref[2:3, :] + cb_ref[...]

    h = h_s[...]
    a = conv(_dot(h, wa_ref[...]), cwa_ref, cba_ref)
    g = conv(_dot(h, wg_ref[...]), cwg_ref, cbg_ref)
    act = (_silu(a) * g).astype(BF16)
    acc_s[...] += _dot(act, wd_ref[...])

    @pl.when(j == pl.num_programs(1) - 1)
    def _():
        out_ref[...] = x_ref[...] + g2_ref[0] * _rms(acc_s[...], gpost_ref[...])


def _ffn(x, sc, sh, g2, lw, seq):
    t = x.shape[0]
    nb = sc.shape[0]
    nblk = D_FF // FFN_BLOCK
    tiles_per_batch = max(seq // FFN_ROWS, 1)

    def mod_map(i, j):
        return ((i // tiles_per_batch) if nb > 1 else 0, 0, 0)

    row = pl.BlockSpec((FFN_ROWS, D_MODEL), lambda i, j: (i, 0))
    mod = pl.BlockSpec((1, 1, D_MODEL), mod_map)
    vec = pl.BlockSpec((1, D_MODEL), lambda i, j: (0, 0))
    return pl.pallas_call(
        functools.partial(_ffn_kernel, seq=seq),
        grid=(t // FFN_ROWS, nblk),
        in_specs=[row, mod, mod, mod, vec, vec,
                  pl.BlockSpec((D_MODEL, FFN_BLOCK), lambda i, j: (0, j)),
                  pl.BlockSpec((D_MODEL, FFN_BLOCK), lambda i, j: (0, j + nblk)),
                  pl.BlockSpec((3, FFN_BLOCK), lambda i, j: (0, j)),
                  pl.BlockSpec((3, FFN_BLOCK), lambda i, j: (0, j + nblk)),
                  pl.BlockSpec((1, FFN_BLOCK), lambda i, j: (0, j)),
                  pl.BlockSpec((1, FFN_BLOCK), lambda i, j: (0, j + nblk)),
                  pl.BlockSpec((FFN_BLOCK, D_MODEL), lambda i, j: (j, 0))],
        out_specs=row,
        out_shape=jax.ShapeDtypeStruct((t, D_MODEL), F32),
        scratch_shapes=[pltpu.VMEM((FFN_ROWS, D_MODEL), BF16), pltpu.VMEM((FFN_ROWS, D_MODEL), F32)],
        compiler_params=_params(("parallel", "arbitrary")),
        name="conv_mlp",
    )(x, sc, sh, g2, lw["g_pre_ffn"], lw["g_post_ffn"],
      lw["w_up"], lw["w_up"], lw["conv_ffn_w"], lw["conv_ffn_w"], lw["conv_ffn_b"], lw["conv_ffn_b"],
      lw["w_down"])


def _pack_in_proj(w_in, w_gate):
    d = w_in.shape[0]
    zeros = lambda n: jnp.zeros((d, n), w_in.dtype)
    parts = [
        w_in[:, :OFF_DT],
        w_in[:, OFF_DT:OFF_QD], zeros(DT_PAD - 2 * SSD_HEADS),
        w_in[:, OFF_QD:OFF_KVD],
        w_in[:, OFF_KVD:OFF_FFT], zeros(KV_PAD - KV_LORA - ROPE_DIM),
        w_in[:, OFF_FFT:],
        w_gate,
    ]
    return jnp.concatenate(parts, axis=1).astype(BF16)


def _pad_q_up(w_q_up):
    w = w_q_up.reshape(Q_LORA, MLA_HEADS, QK_NOPE + ROPE_DIM)
    w = jnp.pad(w, ((0, 0), (0, 0), (0, HEAD_SLOT - QK_NOPE - ROPE_DIM)))
    return w.reshape(Q_LORA, MLA_HEADS * HEAD_SLOT).astype(BF16)


def _pad_k_up(w_kv_up):
    w = w_kv_up.reshape(KV_LORA, MLA_HEADS, QK_NOPE + V_DIM)[:, :, :QK_NOPE]
    top = jnp.pad(w, ((0, 0), (0, 0), (0, HEAD_SLOT - QK_NOPE)))
    eye = jnp.eye(LANES, HEAD_SLOT, k=QK_NOPE, dtype=w_kv_up.dtype)
    eye = eye * (jnp.arange(LANES) < ROPE_DIM).astype(w_kv_up.dtype)[:, None]
    bottom = jnp.broadcast_to(eye[:, None, :], (LANES, MLA_HEADS, HEAD_SLOT))
    return jnp.concatenate([top, bottom], axis=0).reshape(KV_PAD, MLA_HEADS * HEAD_SLOT).astype(BF16)


def _v_up(w_kv_up):
    w = w_kv_up.reshape(KV_LORA, MLA_HEADS, QK_NOPE + V_DIM)[:, :, QK_NOPE:]
    return w.reshape(KV_LORA, MLA_HEADS * V_DIM).astype(BF16)


def _rope_tables(seq, lane0):
    rows = seq // GRID_W
    row = jnp.repeat(jnp.arange(rows), GRID_W)
    col = jnp.tile(jnp.arange(GRID_W), rows)
    quarter = ROPE_DIM // 4
    inv = ROPE_THETA ** (-jnp.arange(quarter, dtype=F32) / quarter)
    ang_row = row.astype(F32)[:, None] * inv
    ang_col = col.astype(F32)[:, None] * inv
    cos = jnp.concatenate([jnp.cos(ang_row)] * 2 + [jnp.cos(ang_col)] * 2, axis=1)
    sin = jnp.concatenate([jnp.sin(ang_row)] * 2 + [jnp.sin(ang_col)] * 2, axis=1)
    lo = (jnp.arange(ROPE_DIM) % (2 * quarter)) < quarter
    sin_next = jnp.where(lo, -sin, 0.0)
    sin_prev = jnp.where(lo, 0.0, sin)
    def place(t, fill):
        left = jnp.full((seq, lane0), fill, F32)
        right = jnp.full((seq, LANES - lane0 - ROPE_DIM), fill, F32)
        return jnp.concatenate([left, t, right], axis=1)
    return place(cos, 1.0), place(sin_next, 0.0), place(sin_prev, 0.0)


def _trunk_pass(x, mods, lw, ctx, tabs, batch, seq):
    sh1, sc1, g1, sh2, sc2, g2 = mods
    z, xbc, dt, qd, kvd, f_in, gates = _pre_mix(x, sc1, sh1, lw["g_pre_mix"], lw["w_packed"],
                                                lw["b_gate"], seq)
    latent = ctx is not None
    h0 = ctx[2] if latent else None
    y_ssd, h_final = _ssd_branch(z, xbc, dt, h0, lw, batch, seq, want_state=not latent)
    mla = _mla_branch(qd, kvd, (ctx[0], ctx[1]) if latent else None, tabs["rope"], lw, batch, seq)
    o_mla = mla[0]
    f = _fft_branch(f_in, tabs["cs"], tabs["dl"], batch, seq)
    x1 = _merge(y_ssd, o_mla, f, gates, x, g1, lw["g_post_mix"], lw, seq)
    x2 = _ffn(x1, sc2, sh2, g2, lw, seq)
    if latent:
        return x2, None
    return x2, (mla[1], mla[2], h_final)


def kernel(x_prompt, x_sample, cache_ckv, cache_kpe, state_ssm, c, c_ctx, w_ada, b_ada, g_pre_mix, g_post_mix, g_pre_ffn, g_post_ffn, w_in, w_gate, b_gate, w_o, conv_ssd_w, conv_ssd_b, dt_bias, a_log, d_skip, g_ssd_norm, w_ssd_out, g_q_norm, w_q_up, g_kv_norm, w_kv_up, w_mla_out, w_fft_out, w_up, conv_ffn_w, conv_ffn_b, w_down):
    batch, seq, _ = x_prompt.shape
    dec_batch, dec_seq, _ = x_sample.shape
    past = cache_ckv.shape[2]

    cvec = jnp.concatenate([c_ctx[None, :], c, jnp.zeros((8 - 1 - dec_batch, D_MODEL), F32)], axis=0)
    mods = _ada_params(cvec, w_ada, b_ada)

    pad_lane = lambda v: jnp.pad(v, ((0, 0), (0, LANES - v.shape[1])))
    cs, dl_ctx = _dft_tables(seq)
    _, dl_lat = _dft_tables(dec_seq)
    tabs_ctx = {"rope": None, "cs": cs, "dl": dl_ctx}
    tabs_lat = {"rope": _rope_tables(dec_seq, QK_NOPE) + _rope_tables(dec_seq, 0), "cs": cs, "dl": dl_lat}
    cache_kpe_p = jnp.pad(cache_kpe, ((0, 0), (0, 0), (0, 0), (0, LANES - ROPE_DIM)))
    h0_all = state_ssm.reshape(dec_batch, DEPTH, 2, D_INNER, SSD_STATE)

    y_p = x_prompt.reshape(batch * seq, D_MODEL)
    y_s = x_sample.reshape(dec_batch * dec_seq, D_MODEL)
    ckv_l, kpe_l, ssm_l = [], [], []
    for i in range(DEPTH):
        lw = {
            "g_pre_mix": g_pre_mix[i][None], "g_post_mix": g_post_mix[i][None],
            "g_pre_ffn": g_pre_ffn[i][None], "g_post_ffn": g_post_ffn[i][None],
            "w_packed": _pack_in_proj(w_in[i], w_gate[i]), "b_gate": b_gate[i][None],
            "w_o": w_o[i].astype(BF16), "w_ssd_out": w_ssd_out[i].astype(BF16),
            "w_mla_out": w_mla_out[i].astype(BF16), "w_fft_out": w_fft_out[i].astype(BF16),
            "conv_ssd_w": conv_ssd_w[i], "conv_ssd_b": conv_ssd_b[i][None],
            "dt_bias": pad_lane(dt_bias[i].reshape(1, 2 * SSD_HEADS)),
            "a_log": pad_lane(a_log[i].reshape(1, 2 * SSD_HEADS)),
            "d_skip": jnp.repeat(d_skip[i], SSD_HEAD_DIM)[None],
            "g_ssd_norm": g_ssd_norm[i][None],
            "g_q_norm": g_q_norm[i][None], "g_kv_norm": g_kv_norm[i][None],
            "w_q_pad": _pad_q_up(w_q_up[i]), "w_k_pad": _pad_k_up(w_kv_up[i]), "w_v": _v_up(w_kv_up[i]),
            "w_up": w_up[i].astype(BF16), "conv_ffn_w": conv_ffn_w[i], "conv_ffn_b": conv_ffn_b[i][None],
            "w_down": w_down[i].astype(BF16),
        }
        m = mods[i]
        mods_ctx = tuple(m[0:1, None, k * D_MODEL:(k + 1) * D_MODEL] for k in range(6))
        mods_lat = tuple(m[1:1 + dec_batch, None, k * D_MODEL:(k + 1) * D_MODEL] for k in range(6))
        y_p, (ckv, kpe, ssm) = _trunk_pass(y_p, mods_ctx, lw, None, tabs_ctx, batch, seq)
        ckv_l.append(ckv.reshape(batch, seq, KV_LORA))
        kpe_l.append(kpe.reshape(batch, seq, LANES)[:, :, :ROPE_DIM])
        ssm_l.append(ssm.reshape(batch, 2, SSD_HEADS, SSD_HEAD_DIM, SSD_STATE))
        ctx = (cache_ckv[:, i], cache_kpe_p[:, i], h0_all[:, i])
        y_s, _ = _trunk_pass(y_s, mods_lat, lw, ctx, tabs_lat, dec_batch, dec_seq)
    del past
    return (y_p.reshape(batch, seq, D_MODEL), y_s.reshape(dec_batch, dec_seq, D_MODEL),
            jnp.stack(ckv_l, axis=1), jnp.stack(kpe_l, axis=1), jnp.stack(ssm_l, axis=1))
```

```python
import functools
import math

import numpy as np
import jax
import jax.numpy as jnp
from jax import lax
from jax.experimental import pallas as pl
from jax.experimental.pallas import tpu as pltpu

F32 = jnp.float32
BF16 = jnp.bfloat16

D_MODEL = 1024
DEPTH = 4
GRID_W = 64
EPS = 1e-6
SSD_HEADS = 16
SSD_HEAD_DIM = 64
D_INNER = SSD_HEADS * SSD_HEAD_DIM
SSD_GROUPS = 2
SSD_STATE = 128
SSD_CHUNK = 128
CONV_CH = D_INNER + 2 * SSD_GROUPS * SSD_STATE
MLA_HEADS = 16
Q_LORA = 384
KV_LORA = 256
QK_NOPE = 64
ROPE_DIM = 32
V_DIM = 64
ROPE_THETA = 10000.0
ATTN_SCALE = (QK_NOPE + ROPE_DIM) ** -0.5
FFT_GROUPS = 4
FFT_GROUP_W = 256
D_FF = 2816
N_BRANCH = 3
OFF_XBC = D_INNER
OFF_DT = OFF_XBC + CONV_CH
OFF_QD = OFF_DT + 2 * SSD_HEADS
OFF_KVD = OFF_QD + Q_LORA
OFF_FFT = OFF_KVD + KV_LORA + ROPE_DIM

LANES = 128
HEAD_SLOT = LANES
KV_PAD = KV_LORA + LANES
DT_PAD = LANES
SEG_Z = 0
SEG_XBC = SEG_Z + D_INNER
SEG_DT = SEG_XBC + CONV_CH
SEG_QD = SEG_DT + DT_PAD
SEG_KVD = SEG_QD + Q_LORA
SEG_FFT = SEG_KVD + KV_PAD
SEG_GATE = SEG_FFT + D_MODEL
PACKED_COLS = SEG_GATE + N_BRANCH * D_MODEL

ROW_TILE = 256
FFN_ROWS = 1024
FFN_BLOCK = 256
Q_BLOCK_ROWS = 256
VMEM_LIMIT = 56 * 1024 * 1024
NEG_BIG = -1e30
LOG2E = math.log2(math.e)


def _params(sem, vmem=VMEM_LIMIT):
    return pltpu.CompilerParams(dimension_semantics=sem, vmem_limit_bytes=vmem)


def _sigmoid(x):
    return 1.0 / (1.0 + jnp.exp(-x))


def _silu(x):
    return x * _sigmoid(x)


def _rms(x, g):
    return x * lax.rsqrt(jnp.mean(x * x, axis=-1, keepdims=True) + EPS) * g


def _dot(a, b):
    return jnp.dot(a, b, preferred_element_type=F32)


def _dot_nt(a, b):
    return lax.dot_general(a, b, (((1,), (1,)), ((), ())), preferred_element_type=F32)


def _const_spec(shape):
    nd = len(shape)
    return pl.BlockSpec(shape, lambda *_: (0,) * nd, pipeline_mode=pl.Buffered(1))


def _ada_kernel(c_ref, w_ref, b_ref, o_ref):
    c = c_ref[...]
    s = _silu(c).astype(BF16)
    o_ref[0] = _dot(s, w_ref[0].astype(BF16)) + b_ref[0]


def _ada_params(cvec, w_ada, b_ada):
    n_chunk = 6
    return pl.pallas_call(
        _ada_kernel,
        grid=(DEPTH, n_chunk),
        in_specs=[
            pl.BlockSpec((8, D_MODEL), lambda l, j: (0, 0)),
            pl.BlockSpec((1, D_MODEL, D_MODEL), lambda l, j: (l, 0, j)),
            pl.BlockSpec((1, 1, D_MODEL), lambda l, j: (l, 0, j)),
        ],
        out_specs=pl.BlockSpec((1, 8, D_MODEL), lambda l, j: (l, 0, j)),
        out_shape=jax.ShapeDtypeStruct((DEPTH, 8, n_chunk * D_MODEL), F32),
        compiler_params=_params(("parallel", "parallel")),
        name="ada_params",
    )(cvec, w_ada, b_ada.reshape(DEPTH, 1, n_chunk * D_MODEL))


def _pre_kernel(x_ref, sc_ref, sh_ref, g_ref, w_ref, bg_ref,
                z_ref, xbc_ref, dt_ref, qd_ref, kvd_ref, fin_ref, gate_ref):
    x = x_ref[...]
    h = (_rms(x, g_ref[...]) * (1.0 + sc_ref[0]) + sh_ref[0]).astype(BF16)

    def seg(out_ref, start, width, bias_ref=None):
        step = 512
        for c0 in range(0, width, step):
            w = min(step, width - c0)
            r = _dot(h, w_ref[:, start + c0:start + c0 + w])
            if bias_ref is not None:
                r = _sigmoid(r + bias_ref[:, c0:c0 + w])
            out_ref[:, c0:c0 + w] = r.astype(out_ref.dtype)

    seg(z_ref, SEG_Z, D_INNER)
    seg(xbc_ref, SEG_XBC, CONV_CH)
    seg(dt_ref, SEG_DT, DT_PAD)
    seg(qd_ref, SEG_QD, Q_LORA)
    seg(kvd_ref, SEG_KVD, KV_PAD)
    seg(fin_ref, SEG_FFT, D_MODEL)
    seg(gate_ref, SEG_GATE, N_BRANCH * D_MODEL, bg_ref)


def _pre_mix(x, sc, sh, g, w_packed, b_gate, seq):
    t = x.shape[0]
    nb = sc.shape[0]
    rows_per_batch = seq // ROW_TILE

    def mod_map(i):
        return ((i // rows_per_batch) if nb > 1 else 0, 0, 0)

    def row_spec(width):
        return pl.BlockSpec((ROW_TILE, width), lambda i: (i, 0))

    widths = (D_INNER, CONV_CH, DT_PAD, Q_LORA, KV_PAD, D_MODEL, N_BRANCH * D_MODEL)
    dtypes = (F32, F32, F32, F32, F32, BF16, F32)
    return pl.pallas_call(
        _pre_kernel,
        grid=(t // ROW_TILE,),
        in_specs=[
            row_spec(D_MODEL),
            pl.BlockSpec((1, 1, D_MODEL), mod_map),
            pl.BlockSpec((1, 1, D_MODEL), mod_map),
            _const_spec((1, D_MODEL)),
            _const_spec((D_MODEL, PACKED_COLS)),
            _const_spec((1, N_BRANCH * D_MODEL)),
        ],
        out_specs=[row_spec(w) for w in widths],
        out_shape=[jax.ShapeDtypeStruct((t, w), d) for w, d in zip(widths, dtypes)],
        compiler_params=_params(("parallel",)),
        name="pre_mix",
    )(x, sc, sh, g, w_packed, b_gate)


def _split3(x):
    hi = x.astype(BF16)
    r1 = x - hi.astype(F32)
    mid = r1.astype(BF16)
    lo = (r1 - mid.astype(F32)).astype(BF16)
    return hi, mid, lo


def _ssd_kernel(*refs, seq, has_h0, want_state):
    it = iter(refs)
    z_ref, xbc_ref, dt_ref = next(it), next(it), next(it)
    h0_ref = next(it) if has_h0 else None
    cw_ref, cb_ref, dtb_ref, alog_ref, dsk_ref, gn_ref = (next(it) for _ in range(6))
    y_ref = next(it)
    hf_ref = next(it) if want_state else None
    xc_s, xs_s, y_s, dt_s, da_s, h_s = (next(it) for _ in range(6))

    nc = seq // SSD_CHUNK
    rb = SSD_CHUNK

    cstep = 512
    for r0 in range(0, seq, rb):
        for c0 in range(0, CONV_CH, cstep):
            cs = slice(c0, c0 + cstep)
            cur = xbc_ref[r0:r0 + rb, cs]
            rid = lax.broadcasted_iota(jnp.int32, (rb, cstep), 0)
            prev = pltpu.roll(cur, 1, 0)
            if r0 > 0:
                prev = jnp.where(rid == 0, xbc_ref[r0 - 1:r0, cs], prev)
            else:
                prev = jnp.where(rid == 0, 0.0, prev)
            nxt = pltpu.roll(cur, rb - 1, 0)
            if r0 + rb < seq:
                nxt = jnp.where(rid == rb - 1, xbc_ref[r0 + rb:r0 + rb + 1, cs], nxt)
            else:
                nxt = jnp.where(rid == rb - 1, 0.0, nxt)
            u = prev * cw_ref[0:1, cs] + cur * cw_ref[1:2, cs] + nxt * cw_ref[2:3, cs] + cb_ref[:, cs]
            u = _silu(u)
            xc_s[r0:r0 + rb, cs] = u
            if c0 < D_INNER:
                xs_s[r0:r0 + rb, cs] = u.astype(BF16)
                y_s[r0:r0 + rb, cs] = u * dsk_ref[:, cs]

    a_row = -jnp.exp(alog_ref[...])
    for r0 in range(0, seq, rb):
        v = dt_ref[r0:r0 + rb, :] + dtb_ref[...]
        dtv = jnp.maximum(v, 0.0) + jnp.log1p(jnp.exp(-jnp.abs(v)))
        dt_s[r0:r0 + rb, :] = dtv
        da_s[r0:r0 + rb, :] = dtv * a_row

    for d in range(2):
        if has_h0:
            for hp in range(SSD_HEADS // 2):
                blk = h0_ref[0, d, hp * LANES:(hp + 1) * LANES, :]
                h_s[d, :, hp * LANES:(hp + 1) * LANES] = blk.T
        else:
            h_s[d] = jnp.zeros((SSD_STATE, D_INNER), F32)

    ri = lax.broadcasted_iota(jnp.int32, (rb, rb), 0)
    ci = lax.broadcasted_iota(jnp.int32, (rb, rb), 1)
    lower = ri >= ci
    upper = ri <= ci
    tri_f = jnp.where(lower, 1.0, 0.0).astype(BF16)
    tri_b = jnp.where(upper, 1.0, 0.0).astype(BF16)
    first_half = ci < SSD_HEAD_DIM

    def chunk_body(step, carry):
        for d in range(2):
            c = step if d == 0 else nc - 1 - step
            r = pl.multiple_of(c * rb, rb)
            mask = lower if d == 0 else upper
            tri = tri_f if d == 0 else tri_b
            end = rb - 1 if d == 0 else 0
            dtc = dt_s[pl.ds(r, rb), :]
            dac = da_s[pl.ds(r, rb), :]
            hi, mid, lo = _split3(dac)
            acum = _dot(tri, hi) + _dot(tri, mid) + _dot(tri, lo)
            acum_t = acum.T
            dt_t = dtc.T
            w_t = dt_t * jnp.exp(acum_t[:, end:end + 1] - acum_t)
            for g in range(SSD_GROUPS):
                b_g = xc_s[pl.ds(r, rb), D_INNER + g * SSD_STATE:D_INNER + (g + 1) * SSD_STATE]
                c_g = xc_s[pl.ds(r, rb),
                           D_INNER + (SSD_GROUPS + g) * SSD_STATE:D_INNER + (SSD_GROUPS + g + 1) * SSD_STATE]
                cb = _dot_nt(c_g.astype(BF16), b_g.astype(BF16))
                b_t = b_g.T
                for hp in range(g * 4, g * 4 + 4):
                    ls = slice(hp * LANES, (hp + 1) * LANES)
                    x_pair = xs_s[pl.ds(r, rb), ls]
                    h_pair = h_s[d, :, ls]
                    rhs = jnp.concatenate([x_pair, h_pair.astype(BF16)], axis=0)
                    ys, ss, cds = [], [], []
                    for e in (2 * hp, 2 * hp + 1):
                        k = d * SSD_HEADS + e
                        colb = jnp.broadcast_to(acum[:, k:k + 1], (rb, rb))
                        seg = jnp.where(mask, colb - acum_t[k:k + 1, :], NEG_BIG)
                        m_e = jnp.exp(seg) * cb * dt_t[k:k + 1, :]
                        e_col = jnp.exp(colb)
                        cw = c_g * e_col
                        lhs = jnp.concatenate([m_e.astype(BF16), cw.astype(BF16)], axis=1)
                        ys.append(_dot(lhs, rhs))
                        bw_t = (b_t * w_t[k:k + 1, :]).astype(BF16)
                        ss.append(_dot(bw_t, x_pair))
                        cds.append(e_col[end:end + 1, :])
                    y_pair = jnp.where(first_half, ys[0], ys[1])
                    y_s[pl.ds(r, rb), ls] = y_s[pl.ds(r, rb), ls] + y_pair
                    cd = jnp.where(first_half[0:1, :], cds[0], cds[1])
                    h_s[d, :, ls] = h_pair * cd + jnp.where(first_half, ss[0], ss[1])
        return carry

    lax.fori_loop(0, nc, chunk_body, 0)

    for r0 in range(0, seq, rb):
        y = y_s[r0:r0 + rb, :] * _silu(z_ref[r0:r0 + rb, :])
        y_ref[r0:r0 + rb, :] = _rms(y, gn_ref[...]).astype(y_ref.dtype)

    if want_state:
        for d in range(2):
            for hp in range(SSD_HEADS // 2):
                hf_ref[0, d, hp * LANES:(hp + 1) * LANES, :] = h_s[d, :, hp * LANES:(hp + 1) * LANES].T


def _ssd_branch(z, xbc, dt, h0, lw, batch, seq, want_state):
    has_h0 = h0 is not None
    row = lambda w: pl.BlockSpec((seq, w), lambda b: (b, 0))
    state_spec = pl.BlockSpec((1, 2, D_INNER, SSD_STATE), lambda b: (b, 0, 0, 0))
    in_specs = [row(D_INNER), row(CONV_CH), row(DT_PAD)]
    args = [z, xbc, dt]
    if has_h0:
        in_specs.append(state_spec)
        args.append(h0)
    consts = [lw["conv_ssd_w"], lw["conv_ssd_b"], lw["dt_bias"], lw["a_log"], lw["d_skip"], lw["g_ssd_norm"]]
    in_specs += [_const_spec(c.shape) for c in consts]
    args += consts
    out_specs = [row(D_INNER)]
    out_shape = [jax.ShapeDtypeStruct((batch * seq, D_INNER), BF16)]
    if want_state:
        out_specs.append(state_spec)
        out_shape.append(jax.ShapeDtypeStruct((batch, 2, D_INNER, SSD_STATE), F32))
    res = pl.pallas_call(
        functools.partial(_ssd_kernel, seq=seq, has_h0=has_h0, want_state=want_state),
        grid=(batch,),
        in_specs=in_specs,
        out_specs=out_specs,
        out_shape=out_shape,
        scratch_shapes=[
            pltpu.VMEM((seq, CONV_CH), F32),
            pltpu.VMEM((seq, D_INNER), BF16),
            pltpu.VMEM((seq, D_INNER), F32),
            pltpu.VMEM((seq, DT_PAD), F32),
            pltpu.VMEM((seq, DT_PAD), F32),
            pltpu.VMEM((2, SSD_STATE, D_INNER), F32),
        ],
        compiler_params=_params(("parallel",)),
        name="ssd_branch",
    )(*args)
    return (res[0], res[1]) if want_state else (res[0], None)


def _mla_kernel(*refs, seq, n_cache):
    it = iter(refs)
    qd_ref, kvd_ref = next(it), next(it)
    latent = n_cache > 0
    qtab_ref = next(it)
    if latent:
        cckv_ref, ckpe_ref = next(it), next(it)
        kcos_ref, ksn_ref, ksp_ref = (next(it) for _ in range(3))
    gq_ref, gkv_ref, wq_ref, wk_ref, wv_ref = (next(it) for _ in range(5))
    o_ref = next(it)
    if not latent:
        ckv_out_ref, kpe_out_ref = next(it), next(it)
    q_s, k_s, v_s = (next(it) for _ in range(3))
    if latent:
        o_s, s_a, s_b, m_a, m_b, p_a, p_b, l_a, l_b = (next(it) for _ in range(9))

    rb = Q_BLOCK_ROWS
    n_pairs = MLA_HEADS // 2

    for r0 in range(0, seq, rb):
        qn = _rms(qd_ref[r0:r0 + rb, :], gq_ref[...]).astype(BF16)
        for h in range(MLA_HEADS):
            qh = _dot(qn, wq_ref[:, h * HEAD_SLOT:(h + 1) * HEAD_SLOT])
            tab = qtab_ref[r0:r0 + rb, :] if latent else qtab_ref[...]
            q_s[h, r0:r0 + rb, :] = (qh * tab).astype(BF16)

    def put_keys(row0, ckv_n, kpe_tile):
        kpe_slot = pltpu.roll(kpe_tile, QK_NOPE, 1) + pltpu.roll(kpe_tile, QK_NOPE + ROPE_DIM, 1)
        cb16 = ckv_n.astype(BF16)
        for h in range(MLA_HEADS):
            kh = _dot(cb16, wk_ref[:, h * HEAD_SLOT:(h + 1) * HEAD_SLOT]) + kpe_slot
            k_s[h, row0:row0 + rb, :] = kh.astype(BF16)
        for hp in range(n_pairs):
            vh = _dot(cb16, wv_ref[:, hp * LANES:(hp + 1) * LANES])
            v_s[hp, row0:row0 + rb, :] = vh.astype(BF16)

    for r0 in range(0, seq, rb):
        ckv_n = _rms(kvd_ref[r0:r0 + rb, 0:KV_LORA], gkv_ref[...])
        kpe = kvd_ref[r0:r0 + rb, KV_LORA:KV_PAD]
        if latent:
            kpe = (kpe * kcos_ref[r0:r0 + rb, :]
                   + pltpu.roll(kpe, LANES - 8, 1) * ksn_ref[r0:r0 + rb, :]
                   + pltpu.roll(kpe, 8, 1) * ksp_ref[r0:r0 + rb, :])
        else:
            ckv_out_ref[r0:r0 + rb, :] = ckv_n
            kpe_out_ref[r0:r0 + rb, :] = kpe
        put_keys(r0, ckv_n, kpe)
    for r0 in range(0, n_cache, rb):
        put_keys(seq + r0, cckv_ref[0, r0:r0 + rb, :], ckpe_ref[0, r0:r0 + rb, :])

    lane = lax.broadcasted_iota(jnp.int32, (rb, LANES), 1)
    first_half = lane < V_DIM

    if not latent:
        for hp in range(n_pairs):
            outs = []
            for j in range(2):
                h = 2 * hp + j
                s = _dot_nt(q_s[h], k_s[h])
                p = jnp.exp2(s - jnp.max(s, axis=-1, keepdims=True))
                l = jnp.sum(p, axis=-1, keepdims=True)
                outs.append(_dot(p.astype(BF16), v_s[hp]) * (1.0 / l))
            o_ref[:, hp * LANES:(hp + 1) * LANES] = jnp.where(first_half, outs[0], outs[1]).astype(BF16)
        return

    nqb = seq // rb
    n_units = n_pairs * nqb

    def unit(u):
        if isinstance(u, int):
            return u // nqb, (u % nqb) * rb
        return u // nqb, pl.multiple_of((u % nqb) * rb, rb)

    def scores(u, s_ref, m_ref):
        hp, r = unit(u)
        for j in range(2):
            s = _dot_nt(q_s[2 * hp + j, pl.ds(r, rb), :], k_s[2 * hp + j])
            s_ref[j] = s
            m_ref[j] = jnp.max(s, axis=-1, keepdims=True)

    def probs(s_ref, m_ref, p_ref, l_ref):
        for j in range(2):
            p = jnp.exp2(s_ref[j] - m_ref[j])
            l_ref[j] = jnp.sum(p, axis=-1, keepdims=True)
            p_ref[j] = p.astype(BF16)

    def values(u, p_ref, l_ref):
        hp, r = unit(u)
        v_pair = v_s[hp]
        o0 = _dot(p_ref[0], v_pair) * (1.0 / l_ref[0])
        o1 = _dot(p_ref[1], v_pair) * (1.0 / l_ref[1])
        o_s[hp, pl.ds(r, rb), :] = jnp.where(first_half, o0, o1).astype(BF16)

    scores(0, s_a, m_a)
    scores(1, s_b, m_b)
    probs(s_a, m_a, p_a, l_a)

    def body(j, carry):
        u = 2 * j
        scores(u, s_a, m_a)
        probs(s_b, m_b, p_b, l_b)
        values(u - 2, p_a, l_a)
        scores(u + 1, s_b, m_b)
        probs(s_a, m_a, p_a, l_a)
        values(u - 1, p_b, l_b)
        return carry

    lax.fori_loop(1, n_units // 2, body, 0)
    probs(s_b, m_b, p_b, l_b)
    values(n_units - 2, p_a, l_a)
    values(n_units - 1, p_b, l_b)

    for hp in range(n_pairs):
        o_ref[:, hp * LANES:(hp + 1) * LANES] = o_s[hp]


def _mla_branch(qd, kvd, cache, rope_tabs, lw, batch, seq):
    latent = cache is not None
    n_cache = cache[0].shape[1] if latent else 0
    row = lambda w: pl.BlockSpec((seq, w), lambda b: (b, 0))
    in_specs = [row(Q_LORA), row(KV_PAD), _const_spec(rope_tabs[0].shape)]
    args = [qd, kvd, rope_tabs[0]]
    if latent:
        in_specs += [pl.BlockSpec((1, n_cache, KV_LORA), lambda b: (b, 0, 0)),
                     pl.BlockSpec((1, n_cache, LANES), lambda b: (b, 0, 0))]
        args += list(cache)
        in_specs += [_const_spec((seq, LANES))] * 3
        args += list(rope_tabs[1:])
    consts = [lw["g_q_norm"], lw["g_kv_norm"], lw["w_q_pad"], lw["w_k_pad"], lw["w_v"]]
    in_specs += [_const_spec(c.shape) for c in consts]
    args += consts
    out_specs = [row(D_MODEL)]
    out_shape = [jax.ShapeDtypeStruct((batch * seq, D_MODEL), BF16)]
    if not latent:
        out_specs += [row(KV_LORA), row(LANES)]
        out_shape += [jax.ShapeDtypeStruct((batch * seq, KV_LORA), F32),
                      jax.ShapeDtypeStruct((batch * seq, LANES), F32)]
    n_keys = seq + n_cache
    scratch = [
        pltpu.VMEM((MLA_HEADS, seq, HEAD_SLOT), BF16),
        pltpu.VMEM((MLA_HEADS, n_keys, HEAD_SLOT), BF16),
        pltpu.VMEM((MLA_HEADS // 2, n_keys, LANES), BF16),
    ]
    if latent:
        tile = (2, Q_BLOCK_ROWS, n_keys)
        stat = (2, Q_BLOCK_ROWS, 1)
        scratch += [pltpu.VMEM((MLA_HEADS // 2, seq, LANES), BF16),
                    pltpu.VMEM(tile, F32), pltpu.VMEM(tile, F32),
                    pltpu.VMEM(stat, F32), pltpu.VMEM(stat, F32),
                    pltpu.VMEM(tile, BF16), pltpu.VMEM(tile, BF16),
                    pltpu.VMEM(stat, F32), pltpu.VMEM(stat, F32)]
    return pl.pallas_call(
        functools.partial(_mla_kernel, seq=seq, n_cache=n_cache),
        grid=(batch,),
        in_specs=in_specs,
        out_specs=out_specs,
        out_shape=out_shape,
        scratch_shapes=scratch,
        compiler_params=_params(("parallel",)),
        name="mla_branch",
    )(*args)


def _fft_kernel(f_ref, cs_ref, dl_ref, o_ref, *, seq):
    scale = 1.0 / math.sqrt(seq * FFT_GROUP_W)
    for g in range(FFT_GROUPS):
        gs = slice(g * FFT_GROUP_W, (g + 1) * FFT_GROUP_W)
        t = _dot(f_ref[:, gs], cs_ref[...]).astype(BF16)
        stack = jnp.concatenate([t[:, :FFT_GROUP_W], t[:, FFT_GROUP_W:]], axis=0)
        o_ref[:, gs] = (_dot(dl_ref[...], stack) * scale).astype(o_ref.dtype)


def _fft_branch(f_in, cs, dl, batch, seq):
    row = pl.BlockSpec((seq, D_MODEL), lambda b: (b, 0))
    return pl.pallas_call(
        functools.partial(_fft_kernel, seq=seq),
        grid=(batch,),
        in_specs=[row, _const_spec(cs.shape), _const_spec(dl.shape)],
        out_specs=row,
        out_shape=jax.ShapeDtypeStruct((batch * seq, D_MODEL), BF16),
        compiler_params=_params(("parallel",)),
        name="fft_branch",
    )(f_in, cs, dl)


def _dft_tables(seq):
    def cos_sin(n):
        idx = np.arange(n, dtype=np.int64)
        ang = 2.0 * np.pi * ((idx[:, None] * idx[None, :]) % n) / n
        return np.cos(ang), np.sin(ang)

    cc, sc = cos_sin(FFT_GROUP_W)
    cl, sl = cos_sin(seq)
    cs = np.concatenate([cc, sc], axis=1).astype(np.float32)
    dl = np.concatenate([cl, -sl], axis=1).astype(np.float32)
    return jnp.asarray(cs).astype(BF16), jnp.asarray(dl).astype(BF16)


def _merge_kernel(y_ref, o_ref, f_ref, gate_ref, x_ref, g1_ref, gp_ref,
                  ws_ref, wm_ref, wf_ref, wo_ref, out_ref):
    mix = (gate_ref[:, 0:D_MODEL] * _dot(y_ref[...], ws_ref[...])
           + gate_ref[:, D_MODEL:2 * D_MODEL] * _dot(o_ref[...], wm_ref[...])
           + gate_ref[:, 2 * D_MODEL:3 * D_MODEL] * _dot(f_ref[...], wf_ref[...]))
    out = _dot(mix.astype(BF16), wo_ref[...])
    out_ref[...] = x_ref[...] + g1_ref[0] * _rms(out, gp_ref[...])


def _merge(y, o, f, gates, x, g1, gpost, lw, seq):
    t = x.shape[0]
    nb = g1.shape[0]
    rows_per_batch = seq // ROW_TILE

    def mod_map(i):
        return ((i // rows_per_batch) if nb > 1 else 0, 0, 0)

    row = lambda w: pl.BlockSpec((ROW_TILE, w), lambda i: (i, 0))
    wspec = _const_spec((D_MODEL, D_MODEL))
    return pl.pallas_call(
        _merge_kernel,
        grid=(t // ROW_TILE,),
        in_specs=[row(D_MODEL), row(D_MODEL), row(D_MODEL), row(N_BRANCH * D_MODEL), row(D_MODEL),
                  pl.BlockSpec((1, 1, D_MODEL), mod_map), _const_spec((1, D_MODEL)),
                  wspec, wspec, wspec, wspec],
        out_specs=row(D_MODEL),
        out_shape=jax.ShapeDtypeStruct((t, D_MODEL), F32),
        compiler_params=_params(("parallel",)),
        name="merge",
    )(y, o, f, gates, x, g1, gpost, lw["w_ssd_out"], lw["w_mla_out"], lw["w_fft_out"], lw["w_o"])


FFN_NBLK = D_FF // FFN_BLOCK


def _ffn_kernel(x_ref, sc_ref, sh_ref, g2_ref, gpre_ref, gpost_ref,
                wu_ref, cw_ref, cb_ref, wd_ref,
                out_ref, h_s, acc_s, u_a, u_b, *, seq):
    h_s[...] = (_rms(x_ref[...], gpre_ref[...]) * (1.0 + sc_ref[0]) + sh_ref[0]).astype(BF16)

    rid = lax.broadcasted_iota(jnp.int32, (FFN_ROWS, FFN_BLOCK), 0) & (seq - 1)
    is_first = rid == 0
    is_last = rid == seq - 1

    def up(k, u_ref):
        h = h_s[...]
        u_ref[0] = _dot(h, wu_ref[k])
        u_ref[1] = _dot(h, wu_ref[FFN_NBLK + k])

    def conv(u, idx):
        cw = cw_ref[idx]
        prev = jnp.where(is_first, 0.0, pltpu.roll(u, 1, 0))
        nxt = jnp.where(is_last, 0.0, pltpu.roll(u, FFN_ROWS - 1, 0))
        return prev * cw[0:1, :] + u * cw[1:2, :] + nxt * cw[2:3, :] + cb_ref[idx]

    def down(k, u_ref, first=False):
        a = conv(u_ref[0], k)
        g = conv(u_ref[1], FFN_NBLK + k)
        d = _dot((_silu(a) * g).astype(BF16), wd_ref[k])
        if first:
            acc_s[...] = d
        else:
            acc_s[...] += d

    up(0, u_a)
    up(1, u_b)
    down(0, u_a, first=True)

    def body(i, carry):
        up(2 * i, u_a)
        down(2 * i - 1, u_b)
        up(2 * i + 1, u_b)
        down(2 * i, u_a)
        return carry

    n_pairs = (FFN_NBLK - 1) // 2
    lax.fori_loop(1, n_pairs, body, 0)
    last = 2 * n_pairs
    up(last, u_a)
    down(last - 1, u_b)
    down(last, u_a)

    out_ref[...] = x_ref[...] + g2_ref[0] * _rms(acc_s[...], gpost_ref[...])


def _ffn(x, sc, sh, g2, lw, seq):
    assert FFN_NBLK % 2 == 1
    t = x.shape[0]
    nb = sc.shape[0]
    tiles_per_batch = max(seq // FFN_ROWS, 1)

    def mod_map(i):
        return ((i // tiles_per_batch) if nb > 1 else 0, 0, 0)

    row = pl.BlockSpec((FFN_ROWS, D_MODEL), lambda i: (i, 0))
    mod = pl.BlockSpec((1, 1, D_MODEL), mod_map)
    consts = [lw["g_pre_ffn"], lw["g_post_ffn"], lw["w_up"], lw["conv_ffn_w"], lw["conv_ffn_b"], lw["w_down"]]
    stage = pltpu.VMEM((2, FFN_ROWS, FFN_BLOCK), F32)
    return pl.pallas_call(
        functools.partial(_ffn_kernel, seq=seq),
        grid=(t // FFN_ROWS,),
        in_specs=[row, mod, mod, mod] + [_const_spec(c.shape) for c in consts],
        out_specs=row,
        out_shape=jax.ShapeDtypeStruct((t, D_MODEL), F32),
        scratch_shapes=[pltpu.VMEM((FFN_ROWS, D_MODEL), BF16), pltpu.VMEM((FFN_ROWS, D_MODEL), F32),
                        stage, stage],
        compiler_params=_params(("parallel",)),
        name="conv_mlp",
    )(x, sc, sh, g2, *consts)


def _pack_in_proj(w_in, w_gate):
    d = w_in.shape[0]
    zeros = lambda n: jnp.zeros((d, n), w_in.dtype)
    parts = [
        w_in[:, :OFF_DT],
        w_in[:, OFF_DT:OFF_QD], zeros(DT_PAD - 2 * SSD_HEADS),
        w_in[:, OFF_QD:OFF_KVD],
        w_in[:, OFF_KVD:OFF_FFT], zeros(KV_PAD - KV_LORA - ROPE_DIM),
        w_in[:, OFF_FFT:],
        w_gate,
    ]
    return jnp.concatenate(parts, axis=1).astype(BF16)


def _rope_partner():
    idx = np.arange(ROPE_DIM)
    quarter = ROPE_DIM // 4
    return np.where(idx % (2 * quarter) < quarter, idx + quarter, idx - quarter)


def _pad_q_up(w_q_up):
    w = w_q_up.reshape(Q_LORA, MLA_HEADS, QK_NOPE + ROPE_DIM)
    pe = w[:, :, QK_NOPE:]
    w = jnp.concatenate([w, pe[:, :, _rope_partner()]], axis=2)
    return w.reshape(Q_LORA, MLA_HEADS * HEAD_SLOT).astype(BF16)


def _pad_k_up(w_kv_up):
    w = w_kv_up.reshape(KV_LORA, MLA_HEADS, QK_NOPE + V_DIM)[:, :, :QK_NOPE]
    w = jnp.pad(w, ((0, 0), (0, 0), (0, HEAD_SLOT - QK_NOPE)))
    return w.reshape(KV_LORA, MLA_HEADS * HEAD_SLOT).astype(BF16)


def _v_up(w_kv_up):
    w = w_kv_up.reshape(KV_LORA, MLA_HEADS, QK_NOPE + V_DIM)[:, :, QK_NOPE:]
    return w.reshape(KV_LORA, MLA_HEADS * V_DIM).astype(BF16)


def _rope_tables(seq):
    rows = seq // GRID_W
    row = jnp.repeat(jnp.arange(rows), GRID_W)
    col = jnp.tile(jnp.arange(GRID_W), rows)
    quarter = ROPE_DIM // 4
    inv = ROPE_THETA ** (-jnp.arange(quarter, dtype=F32) / quarter)
    ang_row = row.astype(F32)[:, None] * inv
    ang_col = col.astype(F32)[:, None] * inv
    cos = jnp.concatenate([jnp.cos(ang_row)] * 2 + [jnp.cos(ang_col)] * 2, axis=1)
    sin = jnp.concatenate([jnp.sin(ang_row)] * 2 + [jnp.sin(ang_col)] * 2, axis=1)
    lo = (jnp.arange(ROPE_DIM) % (2 * quarter)) < quarter
    sin_signed = jnp.where(lo, -sin, sin)
    scale = ATTN_SCALE * LOG2E
    q_tab = jnp.concatenate([jnp.full((seq, QK_NOPE), scale, F32), scale * cos, scale * sin_signed], axis=1)
    pad = lambda t, fill: jnp.concatenate([t, jnp.full((seq, LANES - ROPE_DIM), fill, F32)], axis=1)
    return (q_tab, pad(cos, 1.0), pad(jnp.where(lo, -sin, 0.0), 0.0), pad(jnp.where(lo, 0.0, sin), 0.0))


def _ctx_q_table():
    scale = ATTN_SCALE * LOG2E
    return jnp.concatenate([jnp.full((1, QK_NOPE + ROPE_DIM), scale, F32),
                            jnp.zeros((1, HEAD_SLOT - QK_NOPE - ROPE_DIM), F32)], axis=1)


def _trunk_pass(x, mods, lw, ctx, tabs, batch, seq):
    sh1, sc1, g1, sh2, sc2, g2 = mods
    z, xbc, dt, qd, kvd, f_in, gates = _pre_mix(x, sc1, sh1, lw["g_pre_mix"], lw["w_packed"],
                                                lw["b_gate"], seq)
    latent = ctx is not None
    h0 = ctx[2] if latent else None
    y_ssd, h_final = _ssd_branch(z, xbc, dt, h0, lw, batch, seq, want_state=not latent)
    mla = _mla_branch(qd, kvd, (ctx[0], ctx[1]) if latent else None, tabs["rope"], lw, batch, seq)
    o_mla = mla[0]
    f = _fft_branch(f_in, tabs["cs"], tabs["dl"], batch, seq)
    x1 = _merge(y_ssd, o_mla, f, gates, x, g1, lw["g_post_mix"], lw, seq)
    x2 = _ffn(x1, sc2, sh2, g2, lw, seq)
    if latent:
        return x2, None
    return x2, (mla[1], mla[2], h_final)


def kernel(x_prompt, x_sample, cache_ckv, cache_kpe, state_ssm, c, c_ctx, w_ada, b_ada, g_pre_mix, g_post_mix, g_pre_ffn, g_post_ffn, w_in, w_gate, b_gate, w_o, conv_ssd_w, conv_ssd_b, dt_bias, a_log, d_skip, g_ssd_norm, w_ssd_out, g_q_norm, w_q_up, g_kv_norm, w_kv_up, w_mla_out, w_fft_out, w_up, conv_ffn_w, conv_ffn_b, w_down):
    batch, seq, _ = x_prompt.shape
    dec_batch, dec_seq, _ = x_sample.shape
    past = cache_ckv.shape[2]

    cvec = jnp.concatenate([c_ctx[None, :], c, jnp.zeros((8 - 1 - dec_batch, D_MODEL), F32)], axis=0)
    mods = _ada_params(cvec, w_ada, b_ada)

    pad_lane = lambda v: jnp.pad(v, ((0, 0), (0, LANES - v.shape[1])))
    cs, dl_ctx = _dft_tables(seq)
    _, dl_lat = _dft_tables(dec_seq)
    tabs_ctx = {"rope": (_ctx_q_table(),), "cs": cs, "dl": dl_ctx}
    tabs_lat = {"rope": _rope_tables(dec_seq), "cs": cs, "dl": dl_lat}
    cache_kpe_p = jnp.pad(cache_kpe, ((0, 0), (0, 0), (0, 0), (0, LANES - ROPE_DIM)))
    h0_all = state_ssm.reshape(dec_batch, DEPTH, 2, D_INNER, SSD_STATE)

    y_p = x_prompt.reshape(batch * seq, D_MODEL)
    y_s = x_sample.reshape(dec_batch * dec_seq, D_MODEL)
    ckv_l, kpe_l, ssm_l = [], [], []
    for i in range(DEPTH):
        lw = {
            "g_pre_mix": g_pre_mix[i][None], "g_post_mix": g_post_mix[i][None],
            "g_pre_ffn": g_pre_ffn[i][None], "g_post_ffn": g_post_ffn[i][None],
            "w_packed": _pack_in_proj(w_in[i], w_gate[i]), "b_gate": b_gate[i][None],
            "w_o": w_o[i].astype(BF16), "w_ssd_out": w_ssd_out[i].astype(BF16),
            "w_mla_out": w_mla_out[i].astype(BF16), "w_fft_out": w_fft_out[i].astype(BF16),
            "conv_ssd_w": conv_ssd_w[i], "conv_ssd_b": conv_ssd_b[i][None],
            "dt_bias": pad_lane(dt_bias[i].reshape(1, 2 * SSD_HEADS)),
            "a_log": pad_lane(a_log[i].reshape(1, 2 * SSD_HEADS)),
            "d_skip": jnp.repeat(d_skip[i], SSD_HEAD_DIM)[None],
            "g_ssd_norm": g_ssd_norm[i][None],
            "g_q_norm": g_q_norm[i][None], "g_kv_norm": g_kv_norm[i][None],
            "w_q_pad": _pad_q_up(w_q_up[i]), "w_k_pad": _pad_k_up(w_kv_up[i]), "w_v": _v_up(w_kv_up[i]),
            "w_up": w_up[i].astype(BF16).reshape(D_MODEL, 2 * FFN_NBLK, FFN_BLOCK).transpose(1, 0, 2),
            "conv_ffn_w": conv_ffn_w[i].reshape(3, 2 * FFN_NBLK, FFN_BLOCK).transpose(1, 0, 2),
            "conv_ffn_b": conv_ffn_b[i].reshape(2 * FFN_NBLK, 1, FFN_BLOCK),
            "w_down": w_down[i].astype(BF16).reshape(FFN_NBLK, FFN_BLOCK, D_MODEL),
        }
        m = mods[i]
        mods_ctx = tuple(m[0:1, None, k * D_MODEL:(k + 1) * D_MODEL] for k in range(6))
        mods_lat = tuple(m[1:1 + dec_batch, None, k * D_MODEL:(k + 1) * D_MODEL] for k in range(6))
        y_p, (ckv, kpe, ssm) = _trunk_pass(y_p, mods_ctx, lw, None, tabs_ctx, batch, seq)
        ckv_l.append(ckv.reshape(batch, seq, KV_LORA))
        kpe_l.append(kpe.reshape(batch, seq, LANES)[:, :, :ROPE_DIM])
        ssm_l.append(ssm.reshape(batch, 2, SSD_HEADS, SSD_HEAD_DIM, SSD_STATE))
        ctx = (cache_ckv[:, i], cache_kpe_p[:, i], h0_all[:, i])
        y_s, _ = _trunk_pass(y_s, mods_lat, lw, ctx, tabs_lat, dec_batch, dec_seq)
    del past
    return (y_p.reshape(batch, seq, D_MODEL), y_s.reshape(dec_batch, dec_seq, D_MODEL),
            jnp.stack(ckv_l, axis=1), jnp.stack(kpe_l, axis=1), jnp.stack(ssm_l, axis=1))
```

```python
import functools
import math

import numpy as np
import jax
import jax.numpy as jnp
from jax import lax
from jax.experimental import pallas as pl
from jax.experimental.pallas import tpu as pltpu

F32 = jnp.float32
BF16 = jnp.bfloat16

D_MODEL = 1024
DEPTH = 4
GRID_W = 64
EPS = 1e-6
SSD_HEADS = 16
SSD_HEAD_DIM = 64
D_INNER = SSD_HEADS * SSD_HEAD_DIM
SSD_GROUPS = 2
SSD_STATE = 128
SSD_CHUNK = 128
CONV_CH = D_INNER + 2 * SSD_GROUPS * SSD_STATE
MLA_HEADS = 16
Q_LORA = 384
KV_LORA = 256
QK_NOPE = 64
ROPE_DIM = 32
V_DIM = 64
ROPE_THETA = 10000.0
ATTN_SCALE = (QK_NOPE + ROPE_DIM) ** -0.5
FFT_GROUPS = 4
FFT_GROUP_W = 256
D_FF = 2816
N_BRANCH = 3
N_MOD = 6
OFF_XBC = D_INNER
OFF_DT = OFF_XBC + CONV_CH
OFF_QD = OFF_DT + 2 * SSD_HEADS
OFF_KVD = OFF_QD + Q_LORA
OFF_FFT = OFF_KVD + KV_LORA + ROPE_DIM

LANES = 128
SUBLANES = 8
HEAD_SLOT = LANES
KV_PAD = KV_LORA + LANES
DT_PAD = LANES
HEAD_COLS = OFF_DT
TAIL_DT = 0
TAIL_QD = TAIL_DT + DT_PAD
TAIL_KVD = TAIL_QD + Q_LORA
TAIL_FFT = TAIL_KVD + KV_PAD
TAIL_COLS = TAIL_FFT + D_MODEL

ROW_TILE = 256
FFN_ROWS = 1024
FFN_BLOCK = 256
FFN_NBLK = D_FF // FFN_BLOCK
Q_BLOCK_ROWS = 256
VMEM_LIMIT = 56 * 1024 * 1024
NEG_BIG = -1e30
LOG2E = math.log2(math.e)


def _params(sem, vmem=VMEM_LIMIT):
    return pltpu.CompilerParams(dimension_semantics=sem, vmem_limit_bytes=vmem)


def _sigmoid(x):
    return 1.0 / (1.0 + jnp.exp(-x))


def _silu(x):
    return x * _sigmoid(x)


def _rms(x, g):
    return x * lax.rsqrt(jnp.mean(x * x, axis=-1, keepdims=True) + EPS) * g


def _dot(a, b):
    return jnp.dot(a, b, preferred_element_type=F32)


def _dot_nt(a, b):
    return lax.dot_general(a, b, (((1,), (1,)), ((), ())), preferred_element_type=F32)


def _const_spec(shape):
    nd = len(shape)
    return pl.BlockSpec(shape, lambda *_: (0,) * nd, pipeline_mode=pl.Buffered(1))


def _layer_spec(arr, layer):
    shape = (1,) + tuple(arr.shape[1:])
    nd = len(shape)
    return pl.BlockSpec(shape, lambda *_: (layer,) + (0,) * (nd - 1), pipeline_mode=pl.Buffered(1))


def _mod_spec(layer, k):
    return pl.BlockSpec((1, 1, SUBLANES, D_MODEL), lambda *_: (layer, k, 0, 0), pipeline_mode=pl.Buffered(1))


def _mod_row(ref, row):
    return ref[0, 0, pl.ds(row, 1), :]


def _ada_kernel(c_ref, w_ref, b_ref, o_ref):
    c = c_ref[...]
    s = _silu(c).astype(BF16)
    o_ref[0, 0] = _dot(s, w_ref[0].astype(BF16)) + b_ref[0]


def _ada_params(cvec, w_ada, b_ada):
    return pl.pallas_call(
        _ada_kernel,
        grid=(DEPTH, N_MOD),
        in_specs=[
            pl.BlockSpec((SUBLANES, D_MODEL), lambda l, j: (0, 0)),
            pl.BlockSpec((1, D_MODEL, D_MODEL), lambda l, j: (l, 0, j)),
            pl.BlockSpec((1, 1, D_MODEL), lambda l, j: (l, 0, j)),
        ],
        out_specs=pl.BlockSpec((1, 1, SUBLANES, D_MODEL), lambda l, j: (l, j, 0, 0)),
        out_shape=jax.ShapeDtypeStruct((DEPTH, N_MOD, SUBLANES, D_MODEL), F32),
        compiler_params=_params(("parallel", "parallel")),
        name="ada_params",
    )(cvec, w_ada, b_ada.reshape(DEPTH, 1, N_MOD * D_MODEL))


def _pre_kernel(x_ref, sh_ref, sc_ref, g_ref, wh_ref, wt_ref, wg_ref, bg_ref,
                z_ref, xbc_ref, dt_ref, qd_ref, kvd_ref, fin_ref, gate_ref, *, mod_row):
    row = mod_row(pl.program_id(0))
    x = x_ref[...]
    h = (_rms(x, g_ref[0]) * (1.0 + _mod_row(sc_ref, row)) + _mod_row(sh_ref, row)).astype(BF16)

    def seg(out_ref, w_ref, start, width, bias_ref=None):
        step = 512
        for c0 in range(0, width, step):
            w = min(step, width - c0)
            r = _dot(h, w_ref[0, :, start + c0:start + c0 + w])
            if bias_ref is not None:
                r = _sigmoid(r + bias_ref[0, :, c0:c0 + w])
            out_ref[:, c0:c0 + w] = r.astype(out_ref.dtype)

    seg(z_ref, wh_ref, 0, D_INNER)
    seg(xbc_ref, wh_ref, D_INNER, CONV_CH)
    seg(dt_ref, wt_ref, TAIL_DT, DT_PAD)
    seg(qd_ref, wt_ref, TAIL_QD, Q_LORA)
    seg(kvd_ref, wt_ref, TAIL_KVD, KV_PAD)
    seg(fin_ref, wt_ref, TAIL_FFT, D_MODEL)
    seg(gate_ref, wg_ref, 0, N_BRANCH * D_MODEL, bg_ref)


def _row_of_tile(rows_per_batch, latent):
    if latent:
        return lambda i: 1 + i // rows_per_batch
    return lambda i: 0


def _pre_mix(x, mods, pw, layer, seq, latent):
    t = x.shape[0]

    def row_spec(width):
        return pl.BlockSpec((ROW_TILE, width), lambda i: (i, 0))

    widths = (D_INNER, CONV_CH, DT_PAD, Q_LORA, KV_PAD, D_MODEL, N_BRANCH * D_MODEL)
    dtypes = (F32, F32, F32, F32, F32, BF16, F32)
    consts = [pw["g_pre_mix"], pw["w_head"], pw["w_tail"], pw["w_gate"], pw["b_gate"]]
    return pl.pallas_call(
        functools.partial(_pre_kernel, mod_row=_row_of_tile(seq // ROW_TILE, latent)),
        grid=(t // ROW_TILE,),
        in_specs=[row_spec(D_MODEL), _mod_spec(layer, 0), _mod_spec(layer, 1)]
        + [_layer_spec(c, layer) for c in consts],
        out_specs=[row_spec(w) for w in widths],
        out_shape=[jax.ShapeDtypeStruct((t, w), d) for w, d in zip(widths, dtypes)],
        compiler_params=_params(("parallel",)),
        name="pre_mix",
    )(x, mods, mods, *consts)


def _split3(x):
    hi = x.astype(BF16)
    r1 = x - hi.astype(F32)
    mid = r1.astype(BF16)
    lo = (r1 - mid.astype(F32)).astype(BF16)
    return hi, mid, lo


def _ssd_kernel(*refs, seq, has_h0, want_state, chain_state):
    it = iter(refs)
    z_ref, xbc_ref, dt_ref = next(it), next(it), next(it)
    h0_ref = next(it) if has_h0 else None
    cw_ref, cb_ref, dtb_ref, alog_ref, dsk_ref, gn_ref = (next(it) for _ in range(6))
    if chain_state:
        next(it)
    y_ref = next(it)
    hf_ref = next(it) if want_state else None
    xc_s, xs_s, y_s, ld_s, da_s, h_s = (next(it) for _ in range(6))

    nc = seq // SSD_CHUNK
    rb = SSD_CHUNK

    cstep = 512
    for r0 in range(0, seq, rb):
        for c0 in range(0, CONV_CH, cstep):
            cs = slice(c0, c0 + cstep)
            cur = xbc_ref[r0:r0 + rb, cs]
            rid = lax.broadcasted_iota(jnp.int32, (rb, cstep), 0)
            prev = pltpu.roll(cur, 1, 0)
            if r0 > 0:
                prev = jnp.where(rid == 0, xbc_ref[r0 - 1:r0, cs], prev)
            else:
                prev = jnp.where(rid == 0, 0.0, prev)
            nxt = pltpu.roll(cur, rb - 1, 0)
            if r0 + rb < seq:
                nxt = jnp.where(rid == rb - 1, xbc_ref[r0 + rb:r0 + rb + 1, cs], nxt)
            else:
                nxt = jnp.where(rid == rb - 1, 0.0, nxt)
            u = (prev * cw_ref[0, 0:1, cs] + cur * cw_ref[0, 1:2, cs] + nxt * cw_ref[0, 2:3, cs]
                 + cb_ref[0, :, cs])
            u = _silu(u)
            xc_s[r0:r0 + rb, cs] = u
            if c0 < D_INNER:
                xs_s[r0:r0 + rb, cs] = u.astype(BF16)
                y_s[r0:r0 + rb, cs] = u * dsk_ref[0, :, cs]

    a_row = -jnp.exp(alog_ref[0]) * LOG2E
    for r0 in range(0, seq, rb):
        v = dt_ref[r0:r0 + rb, :] + dtb_ref[0]
        dtv = jnp.maximum(v, 0.0) + jnp.log1p(jnp.exp(-jnp.abs(v)))
        ld_s[r0:r0 + rb, :] = jnp.log2(dtv)
        da_s[r0:r0 + rb, :] = dtv * a_row

    for d in range(2):
        if has_h0:
            for hp in range(SSD_HEADS // 2):
                blk = h0_ref[0, 0, d, hp * LANES:(hp + 1) * LANES, :]
                h_s[d, :, hp * LANES:(hp + 1) * LANES] = blk.T
        else:
            h_s[d] = jnp.zeros((SSD_STATE, D_INNER), F32)

    ri = lax.broadcasted_iota(jnp.int32, (rb, rb), 0)
    ci = lax.broadcasted_iota(jnp.int32, (rb, rb), 1)
    lower = ri >= ci
    upper = ri <= ci
    tri_f = jnp.where(lower, 1.0, 0.0).astype(BF16)
    tri_b = jnp.where(upper, 1.0, 0.0).astype(BF16)
    first_half = ci < SSD_HEAD_DIM

    def chunk_body(step, carry):
        for d in range(2):
            c = step if d == 0 else nc - 1 - step
            r = pl.multiple_of(c * rb, rb)
            mask = lower if d == 0 else upper
            tri = tri_f if d == 0 else tri_b
            end = rb - 1 if d == 0 else 0
            hi, mid, lo = _split3(da_s[pl.ds(r, rb), :])
            acum = _dot(tri, hi) + _dot(tri, mid) + _dot(tri, lo)
            acum_t = acum.T
            rl_t = acum_t - ld_s[pl.ds(r, rb), :].T
            w_t = jnp.exp2(acum_t[:, end:end + 1] - rl_t)
            for g in range(SSD_GROUPS):
                b_g = xc_s[pl.ds(r, rb), D_INNER + g * SSD_STATE:D_INNER + (g + 1) * SSD_STATE]
                c_g = xc_s[pl.ds(r, rb),
                           D_INNER + (SSD_GROUPS + g) * SSD_STATE:D_INNER + (SSD_GROUPS + g + 1) * SSD_STATE]
                cb = _dot_nt(c_g.astype(BF16), b_g.astype(BF16))
                b_t = b_g.T
                for hp in range(g * 4, g * 4 + 4):
                    ls = slice(hp * LANES, (hp + 1) * LANES)
                    x_pair = xs_s[pl.ds(r, rb), ls]
                    h_pair = h_s[d, :, ls]
                    rhs = jnp.concatenate([x_pair, h_pair.astype(BF16)], axis=0)
                    ys, ss, cds = [], [], []
                    for e in (2 * hp, 2 * hp + 1):
                        k = d * SSD_HEADS + e
                        colb = jnp.broadcast_to(acum[:, k:k + 1], (rb, rb))
                        m_e = jnp.exp2(jnp.where(mask, colb - rl_t[k:k + 1, :], NEG_BIG)) * cb
                        e_col = jnp.exp2(colb)
                        cw = c_g * e_col
                        lhs = jnp.concatenate([m_e.astype(BF16), cw.astype(BF16)], axis=1)
                        ys.append(_dot(lhs, rhs))
                        bw_t = (b_t * w_t[k:k + 1, :]).astype(BF16)
                        ss.append(_dot(bw_t, x_pair))
                        cds.append(e_col[end:end + 1, :])
                    y_pair = jnp.where(first_half, ys[0], ys[1])
                    y_s[pl.ds(r, rb), ls] = y_s[pl.ds(r, rb), ls] + y_pair
                    cd = jnp.where(first_half[0:1, :], cds[0], cds[1])
                    h_s[d, :, ls] = h_pair * cd + jnp.where(first_half, ss[0], ss[1])
        return carry

    lax.fori_loop(0, nc, chunk_body, 0)

    for r0 in range(0, seq, rb):
        y = y_s[r0:r0 + rb, :] * _silu(z_ref[r0:r0 + rb, :])
        y_ref[r0:r0 + rb, :] = _rms(y, gn_ref[0]).astype(y_ref.dtype)

    if want_state:
        for d in range(2):
            for hp in range(SSD_HEADS // 2):
                hf_ref[0, 0, d, hp * LANES:(hp + 1) * LANES, :] = h_s[d, :, hp * LANES:(hp + 1) * LANES].T


def _ssd_branch(z, xbc, dt, h0, state_buf, pw, layer, batch, seq, want_state):
    has_h0 = h0 is not None
    chain_state = state_buf is not None
    row = lambda w: pl.BlockSpec((seq, w), lambda b: (b, 0))
    state_spec = pl.BlockSpec((1, 1, 2, D_INNER, SSD_STATE), lambda b: (b, layer, 0, 0, 0))
    in_specs = [row(D_INNER), row(CONV_CH), row(DT_PAD)]
    args = [z, xbc, dt]
    if has_h0:
        in_specs.append(state_spec)
        args.append(h0)
    consts = [pw["conv_ssd_w"], pw["conv_ssd_b"], pw["dt_bias"], pw["a_log"], pw["d_skip"], pw["g_ssd_norm"]]
    in_specs += [_layer_spec(c, layer) for c in consts]
    args += consts
    aliases = {}
    if chain_state:
        aliases = {len(args): 1}
        in_specs.append(pl.BlockSpec(memory_space=pl.ANY))
        args.append(state_buf)
    out_specs = [row(D_INNER)]
    out_shape = [jax.ShapeDtypeStruct((batch * seq, D_INNER), BF16)]
    if want_state:
        out_specs.append(state_spec)
        out_shape.append(jax.ShapeDtypeStruct((batch, DEPTH, 2, D_INNER, SSD_STATE), F32))
    res = pl.pallas_call(
        functools.partial(_ssd_kernel, seq=seq, has_h0=has_h0, want_state=want_state,
                          chain_state=chain_state),
        grid=(batch,),
        in_specs=in_specs,
        out_specs=out_specs,
        out_shape=out_shape,
        input_output_aliases=aliases,
        scratch_shapes=[
            pltpu.VMEM((seq, CONV_CH), F32),
            pltpu.VMEM((seq, D_INNER), BF16),
            pltpu.VMEM((seq, D_INNER), F32),
            pltpu.VMEM((seq, DT_PAD), F32),
            pltpu.VMEM((seq, DT_PAD), F32),
            pltpu.VMEM((2, SSD_STATE, D_INNER), F32),
        ],
        compiler_params=_params(("parallel",)),
        name="ssd_branch",
    )(*args)
    return (res[0], res[1]) if want_state else (res[0], None)


def _mla_kernel(*refs, seq, n_cache, chain_kv):
    it = iter(refs)
    qd_ref, kvd_ref = next(it), next(it)
    latent = n_cache > 0
    qtab_ref = next(it)
    if latent:
        cckv_ref, ckpe_ref = next(it), next(it)
        kcos_ref, ksn_ref, ksp_ref = (next(it) for _ in range(3))
    gq_ref, gkv_ref, wq_ref, wk_ref, wv_ref = (next(it) for _ in range(5))
    if chain_kv:
        next(it), next(it)
    o_ref = next(it)
    if not latent:
        ckv_out_ref, kpe_out_ref = next(it), next(it)
    q_s, k_s, v_s = (next(it) for _ in range(3))
    if latent:
        o_s, s_a, s_b, m_a, m_b, p_a, p_b = (next(it) for _ in range(7))

    rb = Q_BLOCK_ROWS
    n_pairs = MLA_HEADS // 2
    lane = lax.broadcasted_iota(jnp.int32, (rb, LANES), 1)
    first_half = lane < V_DIM
    ones_even = jnp.where(lane == V_DIM, 1.0, 0.0)
    ones_odd = jnp.where(lane == 0, 1.0, 0.0)

    for r0 in range(0, seq, rb):
        qn = _rms(qd_ref[r0:r0 + rb, :], gq_ref[0]).astype(BF16)
        for h in range(MLA_HEADS):
            qh = _dot(qn, wq_ref[0, :, h * HEAD_SLOT:(h + 1) * HEAD_SLOT])
            tab = qtab_ref[r0:r0 + rb, :] if latent else qtab_ref[...]
            q_s[h, r0:r0 + rb, :] = (qh * tab).astype(BF16)

    def put_keys(row0, ckv_n, kpe_tile):
        kpe_slot = pltpu.roll(kpe_tile, QK_NOPE, 1) + pltpu.roll(kpe_tile, QK_NOPE + ROPE_DIM, 1)
        cb16 = ckv_n.astype(BF16)
        for h in range(MLA_HEADS):
            hs = slice(h * HEAD_SLOT, (h + 1) * HEAD_SLOT)
            k_s[h, row0:row0 + rb, :] = (_dot(cb16, wk_ref[0, :, hs]) + kpe_slot).astype(BF16)
            if latent:
                ones = ones_even if h % 2 == 0 else ones_odd
                v_s[h, row0:row0 + rb, :] = (_dot(cb16, wv_ref[0, :, hs]) + ones).astype(BF16)
        if not latent:
            for hp in range(n_pairs):
                ps = slice(hp * LANES, (hp + 1) * LANES)
                v_s[hp, row0:row0 + rb, :] = _dot(cb16, wv_ref[0, :, ps]).astype(BF16)

    for r0 in range(0, seq, rb):
        ckv_n = _rms(kvd_ref[r0:r0 + rb, 0:KV_LORA], gkv_ref[0])
        kpe = kvd_ref[r0:r0 + rb, KV_LORA:KV_PAD]
        if latent:
            kpe = (kpe * kcos_ref[r0:r0 + rb, :]
                   + pltpu.roll(kpe, LANES - 8, 1) * ksn_ref[r0:r0 + rb, :]
                   + pltpu.roll(kpe, 8, 1) * ksp_ref[r0:r0 + rb, :])
        else:
            ckv_out_ref[0, 0, r0:r0 + rb, :] = ckv_n
            kpe_out_ref[0, 0, r0:r0 + rb, :] = kpe[:, 0:ROPE_DIM]
        put_keys(r0, ckv_n, kpe)
    for r0 in range(0, n_cache, rb):
        put_keys(seq + r0, cckv_ref[0, 0, r0:r0 + rb, :], ckpe_ref[0, 0, r0:r0 + rb, :])

    def normalise(o0, o1):
        inv0 = 1.0 / o0[:, V_DIM:V_DIM + 1]
        inv1 = 1.0 / o1[:, 0:1]
        return jnp.where(first_half, o0 * inv0, o1 * inv1).astype(BF16)

    if not latent:
        for hp in range(n_pairs):
            outs = []
            for j in range(2):
                h = 2 * hp + j
                s = _dot_nt(q_s[h], k_s[h])
                p = jnp.exp2(s - jnp.max(s, axis=-1, keepdims=True))
                l = jnp.sum(p, axis=-1, keepdims=True)
                outs.append(_dot(p.astype(BF16), v_s[hp]) * (1.0 / l))
            o_ref[:, hp * LANES:(hp + 1) * LANES] = jnp.where(first_half, outs[0], outs[1]).astype(BF16)
        return

    nqb = seq // rb
    n_units = n_pairs * nqb

    def unit(u):
        if isinstance(u, int):
            return u // nqb, (u % nqb) * rb
        return u // nqb, pl.multiple_of((u % nqb) * rb, rb)

    def scores(u, s_ref, m_ref):
        hp, r = unit(u)
        for j in range(2):
            s = _dot_nt(q_s[2 * hp + j, pl.ds(r, rb), :], k_s[2 * hp + j])
            s_ref[j] = s
            m_ref[j] = jnp.max(s, axis=-1, keepdims=True)

    def probs(s_ref, m_ref, p_ref):
        for j in range(2):
            p_ref[j] = jnp.exp2(s_ref[j] - m_ref[j]).astype(BF16)

    def values(u, p_ref):
        hp, r = unit(u)
        o_s[hp, pl.ds(r, rb), :] = normalise(_dot(p_ref[0], v_s[2 * hp]), _dot(p_ref[1], v_s[2 * hp + 1]))

    scores(0, s_a, m_a)
    scores(1, s_b, m_b)
    probs(s_a, m_a, p_a)

    def body(j, carry):
        u = 2 * j
        scores(u, s_a, m_a)
        probs(s_b, m_b, p_b)
        values(u - 2, p_a)
        scores(u + 1, s_b, m_b)
        probs(s_a, m_a, p_a)
        values(u - 1, p_b)
        return carry

    lax.fori_loop(1, n_units // 2, body, 0)
    probs(s_b, m_b, p_b)
    values(n_units - 2, p_a)
    values(n_units - 1, p_b)

    for hp in range(n_pairs):
        o_ref[:, hp * LANES:(hp + 1) * LANES] = o_s[hp]


def _mla_branch(qd, kvd, cache, rope_tabs, kv_bufs, pw, layer, batch, seq):
    latent = cache is not None
    chain_kv = kv_bufs is not None
    n_cache = cache[0].shape[2] if latent else 0
    row = lambda w: pl.BlockSpec((seq, w), lambda b: (b, 0))
    in_specs = [row(Q_LORA), row(KV_PAD), _const_spec(rope_tabs[0].shape)]
    args = [qd, kvd, rope_tabs[0]]
    if latent:
        in_specs += [pl.BlockSpec((1, 1, n_cache, KV_LORA), lambda b: (b, layer, 0, 0)),
                     pl.BlockSpec((1, 1, n_cache, LANES), lambda b: (b, layer, 0, 0))]
        args += list(cache)
        in_specs += [_const_spec((seq, LANES))] * 3
        args += list(rope_tabs[1:])
    consts = [pw["g_q_norm"], pw["g_kv_norm"], pw["w_q_pad"], pw["w_k_pad"],
              pw["w_v_pad"] if latent else pw["w_v"]]
    in_specs += [_layer_spec(c, layer) for c in consts]
    args += consts
    aliases = {}
    if chain_kv:
        aliases = {len(args): 1, len(args) + 1: 2}
        in_specs += [pl.BlockSpec(memory_space=pl.ANY)] * 2
        args += list(kv_bufs)
    out_specs = [row(D_MODEL)]
    out_shape = [jax.ShapeDtypeStruct((batch * seq, D_MODEL), BF16)]
    if not latent:
        out_specs += [pl.BlockSpec((1, 1, seq, KV_LORA), lambda b: (b, layer, 0, 0)),
                      pl.BlockSpec((1, 1, seq, ROPE_DIM), lambda b: (b, layer, 0, 0))]
        out_shape += [jax.ShapeDtypeStruct((batch, DEPTH, seq, KV_LORA), F32),
                      jax.ShapeDtypeStruct((batch, DEPTH, seq, ROPE_DIM), F32)]
    n_keys = seq + n_cache
    scratch = [
        pltpu.VMEM((MLA_HEADS, seq, HEAD_SLOT), BF16),
        pltpu.VMEM((MLA_HEADS, n_keys, HEAD_SLOT), BF16),
        pltpu.VMEM((MLA_HEADS if latent else MLA_HEADS // 2, n_keys, LANES), BF16),
    ]
    if latent:
        tile = (2, Q_BLOCK_ROWS, n_keys)
        stat = (2, Q_BLOCK_ROWS, 1)
        scratch += [pltpu.VMEM((MLA_HEADS // 2, seq, LANES), BF16),
                    pltpu.VMEM(tile, F32), pltpu.VMEM(tile, F32),
                    pltpu.VMEM(stat, F32), pltpu.VMEM(stat, F32),
                    pltpu.VMEM(tile, BF16), pltpu.VMEM(tile, BF16)]
    return pl.pallas_call(
        functools.partial(_mla_kernel, seq=seq, n_cache=n_cache, chain_kv=chain_kv),
        grid=(batch,),
        in_specs=in_specs,
        out_specs=out_specs,
        out_shape=out_shape,
        input_output_aliases=aliases,
        scratch_shapes=scratch,
        compiler_params=_params(("parallel",)),
        name="mla_branch",
    )(*args)


def _fft_kernel(f_ref, cs_ref, dl_ref, o_ref, *, seq):
    scale = 1.0 / math.sqrt(seq * FFT_GROUP_W)
    for g in range(FFT_GROUPS):
        gs = slice(g * FFT_GROUP_W, (g + 1) * FFT_GROUP_W)
        t = _dot(f_ref[:, gs], cs_ref[...]).astype(BF16)
        stack = jnp.concatenate([t[:, :FFT_GROUP_W], t[:, FFT_GROUP_W:]], axis=0)
        o_ref[:, gs] = (_dot(dl_ref[...], stack) * scale).astype(o_ref.dtype)


def _fft_branch(f_in, cs, dl, batch, seq):
    row = pl.BlockSpec((seq, D_MODEL), lambda b: (b, 0))
    return pl.pallas_call(
        functools.partial(_fft_kernel, seq=seq),
        grid=(batch,),
        in_specs=[row, _const_spec(cs.shape), _const_spec(dl.shape)],
        out_specs=row,
        out_shape=jax.ShapeDtypeStruct((batch * seq, D_MODEL), BF16),
        compiler_params=_params(("parallel",)),
        name="fft_branch",
    )(f_in, cs, dl)


def _dft_tables(seq):
    def cos_sin(n):
        idx = np.arange(n, dtype=np.int64)
        ang = 2.0 * np.pi * ((idx[:, None] * idx[None, :]) % n) / n
        return np.cos(ang), np.sin(ang)

    cc, sc = cos_sin(FFT_GROUP_W)
    cl, sl = cos_sin(seq)
    cs = np.concatenate([cc, sc], axis=1).astype(np.float32)
    dl = np.concatenate([cl, -sl], axis=1).astype(np.float32)
    return jnp.asarray(cs).astype(BF16), jnp.asarray(dl).astype(BF16)


def _merge_kernel(y_ref, o_ref, f_ref, gate_ref, x_ref, g1_ref, gp_ref,
                  ws_ref, wm_ref, wf_ref, wo_ref, out_ref, *, mod_row):
    row = mod_row(pl.program_id(0))
    mix = (gate_ref[:, 0:D_MODEL] * _dot(y_ref[...], ws_ref[0])
           + gate_ref[:, D_MODEL:2 * D_MODEL] * _dot(o_ref[...], wm_ref[0])
           + gate_ref[:, 2 * D_MODEL:3 * D_MODEL] * _dot(f_ref[...], wf_ref[0]))
    out = _dot(mix.astype(BF16), wo_ref[0])
    out_ref[...] = x_ref[...] + _mod_row(g1_ref, row) * _rms(out, gp_ref[0])


def _merge(y, o, f, gates, x, mods, pw, layer, seq, latent):
    t = x.shape[0]
    row = lambda w: pl.BlockSpec((ROW_TILE, w), lambda i: (i, 0))
    consts = [pw["g_post_mix"], pw["w_ssd_out"], pw["w_mla_out"], pw["w_fft_out"], pw["w_o"]]
    return pl.pallas_call(
        functools.partial(_merge_kernel, mod_row=_row_of_tile(seq // ROW_TILE, latent)),
        grid=(t // ROW_TILE,),
        in_specs=[row(D_MODEL), row(D_MODEL), row(D_MODEL), row(N_BRANCH * D_MODEL), row(D_MODEL),
                  _mod_spec(layer, 2)] + [_layer_spec(c, layer) for c in consts],
        out_specs=row(D_MODEL),
        out_shape=jax.ShapeDtypeStruct((t, D_MODEL), F32),
        compiler_params=_params(("parallel",)),
        name="merge",
    )(y, o, f, gates, x, mods, *consts)


def _ffn_kernel(x_ref, sh_ref, sc_ref, g2_ref, gpre_ref, gpost_ref,
                wu_ref, cw_ref, cb_ref, wd_ref,
                out_ref, h_s, acc_s, u_a, u_b, *, seq, mod_row):
    row = mod_row(pl.program_id(0))
    h_s[...] = (_rms(x_ref[...], gpre_ref[0]) * (1.0 + _mod_row(sc_ref, row))
                + _mod_row(sh_ref, row)).astype(BF16)

    rid = lax.broadcasted_iota(jnp.int32, (FFN_ROWS, FFN_BLOCK), 0) & (seq - 1)
    is_first = rid == 0
    is_last = rid == seq - 1

    def cols(k, gate_half):
        c0 = (D_FF if gate_half else 0) + k * FFN_BLOCK
        return slice(c0, c0 + FFN_BLOCK)

    def up(k, u_ref):
        h = h_s[...]
        u_ref[0] = _dot(h, wu_ref[0, :, cols(k, False)])
        u_ref[1] = _dot(h, wu_ref[0, :, cols(k, True)])

    def conv(u, cs):
        prev = jnp.where(is_first, 0.0, pltpu.roll(u, 1, 0))
        nxt = jnp.where(is_last, 0.0, pltpu.roll(u, FFN_ROWS - 1, 0))
        return (prev * cw_ref[0, 0:1, cs] + u * cw_ref[0, 1:2, cs] + nxt * cw_ref[0, 2:3, cs]
                + cb_ref[0, :, cs])

    def down(k, u_ref):
        a = conv(u_ref[0], cols(k, False))
        g = conv(u_ref[1], cols(k, True))
        d = _dot((_silu(a) * g).astype(BF16), wd_ref[0, k * FFN_BLOCK:(k + 1) * FFN_BLOCK, :])
        if k == 0:
            acc_s[...] = d
        else:
            acc_s[...] += d

    bufs = (u_a, u_b)
    up(0, u_a)
    for k in range(FFN_NBLK):
        if k + 1 < FFN_NBLK:
            up(k + 1, bufs[(k + 1) % 2])
        down(k, bufs[k % 2])

    out_ref[...] = x_ref[...] + _mod_row(g2_ref, row) * _rms(acc_s[...], gpost_ref[0])


def _ffn(x, mods, pw, layer, seq, latent):
    t = x.shape[0]
    row = pl.BlockSpec((FFN_ROWS, D_MODEL), lambda i: (i, 0))
    consts = [pw["g_pre_ffn"], pw["g_post_ffn"], pw["w_up"], pw["conv_ffn_w"], pw["conv_ffn_b"], pw["w_down"]]
    stage = pltpu.VMEM((2, FFN_ROWS, FFN_BLOCK), F32)
    return pl.pallas_call(
        functools.partial(_ffn_kernel, seq=seq, mod_row=_row_of_tile(max(seq // FFN_ROWS, 1), latent)),
        grid=(t // FFN_ROWS,),
        in_specs=[row, _mod_spec(layer, 3), _mod_spec(layer, 4), _mod_spec(layer, 5)]
        + [_layer_spec(c, layer) for c in consts],
        out_specs=row,
        out_shape=jax.ShapeDtypeStruct((t, D_MODEL), F32),
        scratch_shapes=[pltpu.VMEM((FFN_ROWS, D_MODEL), BF16), pltpu.VMEM((FFN_ROWS, D_MODEL), F32),
                        stage, stage],
        compiler_params=_params(("parallel",)),
        name="conv_mlp",
    )(x, mods, mods, mods, *consts)


def _pack_tail(w_in):
    zeros = lambda n: jnp.zeros(w_in.shape[:2] + (n,), w_in.dtype)
    parts = [
        w_in[:, :, OFF_DT:OFF_QD], zeros(DT_PAD - 2 * SSD_HEADS),
        w_in[:, :, OFF_QD:OFF_KVD],
        w_in[:, :, OFF_KVD:OFF_FFT], zeros(KV_PAD - KV_LORA - ROPE_DIM),
        w_in[:, :, OFF_FFT:],
    ]
    return jnp.concatenate(parts, axis=2).astype(BF16)


def _rope_partner():
    idx = np.arange(ROPE_DIM)
    quarter = ROPE_DIM // 4
    return np.where(idx % (2 * quarter) < quarter, idx + quarter, idx - quarter)


def _pad_q_up(w_q_up):
    w = w_q_up.reshape(DEPTH, Q_LORA, MLA_HEADS, QK_NOPE + ROPE_DIM)
    pe = w[..., QK_NOPE:]
    w = jnp.concatenate([w, pe[..., _rope_partner()]], axis=-1)
    return w.reshape(DEPTH, Q_LORA, MLA_HEADS * HEAD_SLOT).astype(BF16)


def _pad_kv_up(w_kv_up):
    w = w_kv_up.reshape(DEPTH, KV_LORA, MLA_HEADS // 2, 2, QK_NOPE + V_DIM)
    zero = jnp.zeros_like(w[..., :QK_NOPE])
    wk = jnp.concatenate([w[..., :QK_NOPE], zero], axis=-1)
    v = w[..., QK_NOPE:]
    wv = jnp.stack([jnp.concatenate([v[..., 0, :], zero[..., 0, :]], axis=-1),
                    jnp.concatenate([zero[..., 1, :], v[..., 1, :]], axis=-1)], axis=3)
    shape = (DEPTH, KV_LORA, MLA_HEADS * HEAD_SLOT)
    return wk.reshape(shape).astype(BF16), wv.reshape(shape).astype(BF16)


def _rope_tables(seq):
    rows = seq // GRID_W
    row = jnp.repeat(jnp.arange(rows), GRID_W)
    col = jnp.tile(jnp.arange(GRID_W), rows)
    quarter = ROPE_DIM // 4
    inv = ROPE_THETA ** (-jnp.arange(quarter, dtype=F32) / quarter)
    ang_row = row.astype(F32)[:, None] * inv
    ang_col = col.astype(F32)[:, None] * inv
    cos = jnp.concatenate([jnp.cos(ang_row)] * 2 + [jnp.cos(ang_col)] * 2, axis=1)
    sin = jnp.concatenate([jnp.sin(ang_row)] * 2 + [jnp.sin(ang_col)] * 2, axis=1)
    lo = (jnp.arange(ROPE_DIM) % (2 * quarter)) < quarter
    sin_signed = jnp.where(lo, -sin, sin)
    scale = ATTN_SCALE * LOG2E
    q_tab = jnp.concatenate([jnp.full((seq, QK_NOPE), scale, F32), scale * cos, scale * sin_signed], axis=1)
    pad = lambda t, fill: jnp.concatenate([t, jnp.full((seq, LANES - ROPE_DIM), fill, F32)], axis=1)
    return (q_tab, pad(cos, 1.0), pad(jnp.where(lo, -sin, 0.0), 0.0), pad(jnp.where(lo, 0.0, sin), 0.0))


def _ctx_q_table():
    scale = ATTN_SCALE * LOG2E
    return jnp.concatenate([jnp.full((1, QK_NOPE + ROPE_DIM), scale, F32),
                            jnp.zeros((1, HEAD_SLOT - QK_NOPE - ROPE_DIM), F32)], axis=1)


def _trunk_pass(x, mods, pw, layer, ctx, outs, tabs, batch, seq):
    latent = ctx is not None
    z, xbc, dt, qd, kvd, f_in, gates = _pre_mix(x, mods, pw, layer, seq, latent)
    y_ssd, ssm = _ssd_branch(z, xbc, dt, ctx[2] if latent else None, outs[2] if outs else None,
                             pw, layer, batch, seq, want_state=not latent)
    mla = _mla_branch(qd, kvd, (ctx[0], ctx[1]) if latent else None, tabs["rope"],
                      (outs[0], outs[1]) if outs else None, pw, layer, batch, seq)
    f = _fft_branch(f_in, tabs["cs"], tabs["dl"], batch, seq)
    x1 = _merge(y_ssd, mla[0], f, gates, x, mods, pw, layer, seq, latent)
    x2 = _ffn(x1, mods, pw, layer, seq, latent)
    if latent:
        return x2, None
    return x2, (mla[1], mla[2], ssm)


def kernel(x_prompt, x_sample, cache_ckv, cache_kpe, state_ssm, c, c_ctx, w_ada, b_ada, g_pre_mix, g_post_mix, g_pre_ffn, g_post_ffn, w_in, w_gate, b_gate, w_o, conv_ssd_w, conv_ssd_b, dt_bias, a_log, d_skip, g_ssd_norm, w_ssd_out, g_q_norm, w_q_up, g_kv_norm, w_kv_up, w_mla_out, w_fft_out, w_up, conv_ffn_w, conv_ffn_b, w_down):
    batch, seq, _ = x_prompt.shape
    dec_batch, dec_seq, _ = x_sample.shape

    cvec = jnp.concatenate([c_ctx[None, :], c, jnp.zeros((SUBLANES - 1 - dec_batch, D_MODEL), F32)], axis=0)
    mods = _ada_params(cvec, w_ada, b_ada)

    vec = lambda v: v[:, None, :]
    pad_lane = lambda v: jnp.pad(v, ((0, 0), (0, 0), (0, LANES - v.shape[-1])))
    w_k_pad, w_v_pad = _pad_kv_up(w_kv_up)
    pw = {
        "g_pre_mix": vec(g_pre_mix), "g_post_mix": vec(g_post_mix),
        "g_pre_ffn": vec(g_pre_ffn), "g_post_ffn": vec(g_post_ffn),
        "w_head": w_in[:, :, :HEAD_COLS].astype(BF16), "w_tail": _pack_tail(w_in),
        "w_gate": w_gate.astype(BF16), "b_gate": vec(b_gate),
        "w_o": w_o.astype(BF16), "w_ssd_out": w_ssd_out.astype(BF16),
        "w_mla_out": w_mla_out.astype(BF16), "w_fft_out": w_fft_out.astype(BF16),
        "conv_ssd_w": conv_ssd_w, "conv_ssd_b": vec(conv_ssd_b),
        "dt_bias": pad_lane(dt_bias.reshape(DEPTH, 1, 2 * SSD_HEADS)),
        "a_log": pad_lane(a_log.reshape(DEPTH, 1, 2 * SSD_HEADS)),
        "d_skip": vec(jnp.repeat(d_skip, SSD_HEAD_DIM, axis=1)),
        "g_ssd_norm": vec(g_ssd_norm),
        "g_q_norm": vec(g_q_norm), "g_kv_norm": vec(g_kv_norm),
        "w_q_pad": _pad_q_up(w_q_up), "w_k_pad": w_k_pad, "w_v_pad": w_v_pad,
        "w_v": w_kv_up.reshape(DEPTH, KV_LORA, MLA_HEADS, QK_NOPE + V_DIM)[..., QK_NOPE:]
        .reshape(DEPTH, KV_LORA, MLA_HEADS * V_DIM).astype(BF16),
        "w_up": w_up.astype(BF16), "conv_ffn_w": conv_ffn_w, "conv_ffn_b": vec(conv_ffn_b),
        "w_down": w_down.astype(BF16),
    }

    cs, dl_ctx = _dft_tables(seq)
    _, dl_lat = _dft_tables(dec_seq)
    tabs_ctx = {"rope": (_ctx_q_table(),), "cs": cs, "dl": dl_ctx}
    tabs_lat = {"rope": _rope_tables(dec_seq), "cs": cs, "dl": dl_lat}
    cache_kpe_p = jnp.pad(cache_kpe, ((0, 0), (0, 0), (0, 0), (0, LANES - ROPE_DIM)))
    h0_all = state_ssm.reshape(dec_batch, DEPTH, 2, D_INNER, SSD_STATE)
    ctx = (cache_ckv, cache_kpe_p, h0_all)

    y_p = x_prompt.reshape(batch * seq, D_MODEL)
    y_s = x_sample.reshape(dec_batch * dec_seq, D_MODEL)
    outs = None
    for i in range(DEPTH):
        y_p, outs = _trunk_pass(y_p, mods, pw, i, None, outs, tabs_ctx, batch, seq)
        y_s, _ = _trunk_pass(y_s, mods, pw, i, ctx, None, tabs_lat, dec_batch, dec_seq)
    new_ckv, new_kpe, new_ssm = outs
    return (y_p.reshape(batch, seq, D_MODEL), y_s.reshape(dec_batch, dec_seq, D_MODEL),
            new_ckv, new_kpe, new_ssm.reshape(batch, DEPTH, 2, SSD_HEADS, SSD_HEAD_DIM, SSD_STATE))
```

```python
import functools
import math

import numpy as np
import jax
import jax.numpy as jnp
from jax import lax
from jax.experimental import pallas as pl
from jax.experimental.pallas import tpu as pltpu

F32 = jnp.float32
BF16 = jnp.bfloat16

D_MODEL = 1024
DEPTH = 4
GRID_W = 64
EPS = 1e-6
SSD_HEADS = 16
SSD_HEAD_DIM = 64
D_INNER = SSD_HEADS * SSD_HEAD_DIM
SSD_GROUPS = 2
SSD_STATE = 128
SSD_CHUNK = 128
CONV_CH = D_INNER + 2 * SSD_GROUPS * SSD_STATE
MLA_HEADS = 16
Q_LORA = 384
KV_LORA = 256
QK_NOPE = 64
ROPE_DIM = 32
V_DIM = 64
ROPE_THETA = 10000.0
ATTN_SCALE = (QK_NOPE + ROPE_DIM) ** -0.5
FFT_GROUPS = 4
FFT_GROUP_W = 256
D_FF = 2816
N_BRANCH = 3
N_MOD = 6
OFF_XBC = D_INNER
OFF_DT = OFF_XBC + CONV_CH
OFF_QD = OFF_DT + 2 * SSD_HEADS
OFF_KVD = OFF_QD + Q_LORA
OFF_FFT = OFF_KVD + KV_LORA + ROPE_DIM

LANES = 128
SUBLANES = 8
HEAD_SLOT = LANES
KV_PAD = KV_LORA + LANES
DT_PAD = LANES
HEAD_COLS = OFF_DT
TAIL_DT = 0
TAIL_QD = TAIL_DT + DT_PAD
TAIL_KVD = TAIL_QD + Q_LORA
TAIL_FFT = TAIL_KVD + KV_PAD
TAIL_COLS = TAIL_FFT + D_MODEL

ROW_TILE = 256
FFN_ROWS = 1024
FFN_BLOCK = 256
FFN_NBLK = D_FF // FFN_BLOCK
Q_BLOCK_ROWS = 256
VMEM_LIMIT = 56 * 1024 * 1024
NEG_BIG = -1e30
LOG2E = math.log2(math.e)


def _params(sem, vmem=VMEM_LIMIT):
    return pltpu.CompilerParams(dimension_semantics=sem, vmem_limit_bytes=vmem)


def _sigmoid(x):
    return 1.0 / (1.0 + jnp.exp(-x))


def _silu(x):
    return x * _sigmoid(x)


def _rms(x, g):
    return x * lax.rsqrt(jnp.mean(x * x, axis=-1, keepdims=True) + EPS) * g


def _dot(a, b):
    return jnp.dot(a, b, preferred_element_type=F32)


def _dot_nt(a, b):
    return lax.dot_general(a, b, (((1,), (1,)), ((), ())), preferred_element_type=F32)


def _const_spec(shape):
    nd = len(shape)
    return pl.BlockSpec(shape, lambda *_: (0,) * nd, pipeline_mode=pl.Buffered(1))


def _layer_spec(arr, layer):
    shape = (1,) + tuple(arr.shape[1:])
    nd = len(shape)
    return pl.BlockSpec(shape, lambda *_: (layer,) + (0,) * (nd - 1), pipeline_mode=pl.Buffered(1))


def _mod_spec(layer, k):
    return pl.BlockSpec((1, 1, SUBLANES, D_MODEL), lambda *_: (layer, k, 0, 0), pipeline_mode=pl.Buffered(1))


def _mod_row(ref, row):
    return ref[0, 0, pl.ds(row, 1), :]


def _ada_kernel(c_ref, w_ref, b_ref, o_ref):
    c = c_ref[...]
    s = _silu(c).astype(BF16)
    o_ref[0, 0] = _dot(s, w_ref[0].astype(BF16)) + b_ref[0]


def _ada_params(cvec, w_ada, b_ada):
    return pl.pallas_call(
        _ada_kernel,
        grid=(DEPTH, N_MOD),
        in_specs=[
            pl.BlockSpec((SUBLANES, D_MODEL), lambda l, j: (0, 0)),
            pl.BlockSpec((1, D_MODEL, D_MODEL), lambda l, j: (l, 0, j)),
            pl.BlockSpec((1, 1, D_MODEL), lambda l, j: (l, 0, j)),
        ],
        out_specs=pl.BlockSpec((1, 1, SUBLANES, D_MODEL), lambda l, j: (l, j, 0, 0)),
        out_shape=jax.ShapeDtypeStruct((DEPTH, N_MOD, SUBLANES, D_MODEL), F32),
        compiler_params=_params(("parallel", "parallel")),
        name="ada_params",
    )(cvec, w_ada, b_ada.reshape(DEPTH, 1, N_MOD * D_MODEL))


def _pre_kernel(x_ref, sh_ref, sc_ref, g_ref, wh_ref, wt_ref, wg_ref, bg_ref,
                z_ref, xbc_ref, dt_ref, qd_ref, kvd_ref, fin_ref, gate_ref, *, mod_row):
    row = mod_row(pl.program_id(0))
    x = x_ref[...]
    h = (_rms(x, g_ref[0]) * (1.0 + _mod_row(sc_ref, row)) + _mod_row(sh_ref, row)).astype(BF16)

    def seg(out_ref, w_ref, start, width, bias_ref=None):
        step = 512
        for c0 in range(0, width, step):
            w = min(step, width - c0)
            r = _dot(h, w_ref[0, :, start + c0:start + c0 + w])
            if bias_ref is not None:
                r = _sigmoid(r + bias_ref[0, :, c0:c0 + w])
            out_ref[:, c0:c0 + w] = r.astype(out_ref.dtype)

    seg(z_ref, wh_ref, 0, D_INNER)
    seg(xbc_ref, wh_ref, D_INNER, CONV_CH)
    seg(dt_ref, wt_ref, TAIL_DT, DT_PAD)
    seg(qd_ref, wt_ref, TAIL_QD, Q_LORA)
    seg(kvd_ref, wt_ref, TAIL_KVD, KV_PAD)
    seg(fin_ref, wt_ref, TAIL_FFT, D_MODEL)
    seg(gate_ref, wg_ref, 0, N_BRANCH * D_MODEL, bg_ref)


def _row_of_tile(rows_per_batch, latent):
    if latent:
        return lambda i: 1 + i // rows_per_batch
    return lambda i: 0


def _pre_mix(x, mods, pw, layer, seq, latent):
    t = x.shape[0]

    def row_spec(width):
        return pl.BlockSpec((ROW_TILE, width), lambda i: (i, 0))

    widths = (D_INNER, CONV_CH, DT_PAD, Q_LORA, KV_PAD, D_MODEL, N_BRANCH * D_MODEL)
    dtypes = (F32, F32, F32, F32, F32, BF16, F32)
    consts = [pw["g_pre_mix"], pw["w_head"], pw["w_tail"], pw["w_gate"], pw["b_gate"]]
    return pl.pallas_call(
        functools.partial(_pre_kernel, mod_row=_row_of_tile(seq // ROW_TILE, latent)),
        grid=(t // ROW_TILE,),
        in_specs=[row_spec(D_MODEL), _mod_spec(layer, 0), _mod_spec(layer, 1)]
        + [_layer_spec(c, layer) for c in consts],
        out_specs=[row_spec(w) for w in widths],
        out_shape=[jax.ShapeDtypeStruct((t, w), d) for w, d in zip(widths, dtypes)],
        compiler_params=_params(("parallel",)),
        name="pre_mix",
    )(x, mods, mods, *consts)


def _split3(x):
    hi = x.astype(BF16)
    r1 = x - hi.astype(F32)
    mid = r1.astype(BF16)
    lo = (r1 - mid.astype(F32)).astype(BF16)
    return hi, mid, lo


def _ssd_kernel(*refs, seq, has_h0, want_state, chain_state):
    it = iter(refs)
    z_ref, xbc_ref, dt_ref = next(it), next(it), next(it)
    h0_ref = next(it) if has_h0 else None
    cw_ref, cb_ref, dtb_ref, alog_ref, dsk_ref, gn_ref = (next(it) for _ in range(6))
    if chain_state:
        next(it)
    y_ref = next(it)
    hf_ref = next(it) if want_state else None
    bc_s, xs_s, yf_s, yb_s, ld_s, da_s, h_s = (next(it) for _ in range(7))
    lhs_a, lhs_b, bwt_a, bwt_b, cd_a, cd_b = (next(it) for _ in range(6))

    nc = seq // SSD_CHUNK
    rb = SSD_CHUNK

    cstep = 512
    for r0 in range(0, seq, rb):
        for c0 in range(0, CONV_CH, cstep):
            cs = slice(c0, c0 + cstep)
            cur = xbc_ref[r0:r0 + rb, cs]
            rid = lax.broadcasted_iota(jnp.int32, (rb, cstep), 0)
            prev = pltpu.roll(cur, 1, 0)
            if r0 > 0:
                prev = jnp.where(rid == 0, xbc_ref[r0 - 1:r0, cs], prev)
            else:
                prev = jnp.where(rid == 0, 0.0, prev)
            nxt = pltpu.roll(cur, rb - 1, 0)
            if r0 + rb < seq:
                nxt = jnp.where(rid == rb - 1, xbc_ref[r0 + rb:r0 + rb + 1, cs], nxt)
            else:
                nxt = jnp.where(rid == rb - 1, 0.0, nxt)
            u = (prev * cw_ref[0, 0:1, cs] + cur * cw_ref[0, 1:2, cs] + nxt * cw_ref[0, 2:3, cs]
                 + cb_ref[0, :, cs])
            u = _silu(u)
            if c0 < D_INNER:
                xs_s[r0:r0 + rb, cs] = u.astype(BF16)
                yf_s[r0:r0 + rb, cs] = u * dsk_ref[0, :, cs]
            else:
                bc_s[r0:r0 + rb, c0 - D_INNER:c0 - D_INNER + cstep] = u

    a_row = -jnp.exp(alog_ref[0]) * LOG2E
    for r0 in range(0, seq, rb):
        v = dt_ref[r0:r0 + rb, :] + dtb_ref[0]
        dtv = jnp.maximum(v, 0.0) + jnp.log1p(jnp.exp(-jnp.abs(v)))
        ld_s[r0:r0 + rb, :] = jnp.log2(dtv)
        da_s[r0:r0 + rb, :] = dtv * a_row

    for d in range(2):
        if has_h0:
            for hp in range(SSD_HEADS // 2):
                blk = h0_ref[0, 0, d, hp * LANES:(hp + 1) * LANES, :]
                h_s[d, :, hp * LANES:(hp + 1) * LANES] = blk.T
        else:
            h_s[d] = jnp.zeros((SSD_STATE, D_INNER), F32)

    ri = lax.broadcasted_iota(jnp.int32, (rb, rb), 0)
    ci = lax.broadcasted_iota(jnp.int32, (rb, rb), 1)
    lower = ri >= ci
    upper = ri <= ci
    tri_f = jnp.where(lower, 1.0, 0.0).astype(BF16)
    tri_b = jnp.where(upper, 1.0, 0.0).astype(BF16)
    first_half = ci < SSD_HEAD_DIM

    def rows_of(step, d):
        c = step if d == 0 else nc - 1 - step
        return c * rb if isinstance(c, int) else pl.multiple_of(c * rb, rb)

    def build(step, stage):
        lhs_ref, bwt_ref, cd_ref = stage
        for d in range(2):
            r = rows_of(step, d)
            mask = lower if d == 0 else upper
            tri = tri_f if d == 0 else tri_b
            end = rb - 1 if d == 0 else 0
            hi, mid, lo = _split3(da_s[pl.ds(r, rb), :])
            acum = _dot(tri, hi) + _dot(tri, mid) + _dot(tri, lo)
            acum_t = acum.T
            rl_t = acum_t - ld_s[pl.ds(r, rb), :].T
            w_t = jnp.exp2(acum_t[:, end:end + 1] - rl_t)
            for g in range(SSD_GROUPS):
                b_g = bc_s[pl.ds(r, rb), g * SSD_STATE:(g + 1) * SSD_STATE]
                c_g = bc_s[pl.ds(r, rb), (SSD_GROUPS + g) * SSD_STATE:(SSD_GROUPS + g + 1) * SSD_STATE]
                cb = _dot_nt(c_g.astype(BF16), b_g.astype(BF16))
                b_t = b_g.T
                for e in range(g * 8, g * 8 + 8):
                    k = d * SSD_HEADS + e
                    colb = jnp.broadcast_to(acum[:, k:k + 1], (rb, rb))
                    m_e = jnp.exp2(jnp.where(mask, colb - rl_t[k:k + 1, :], NEG_BIG)) * cb
                    e_col = jnp.exp2(colb)
                    lhs_ref[d, e] = jnp.concatenate([m_e.astype(BF16), (c_g * e_col).astype(BF16)], axis=1)
                    bwt_ref[d, e] = (b_t * w_t[k:k + 1, :]).astype(BF16)
                    cd_ref[d, e] = e_col[end:end + 1, :]

    def apply(step, stage):
        lhs_ref, bwt_ref, cd_ref = stage
        for d in range(2):
            r = rows_of(step, d)
            for hp in range(SSD_HEADS // 2):
                ls = slice(hp * LANES, (hp + 1) * LANES)
                x_pair = xs_s[pl.ds(r, rb), ls]
                h_pair = h_s[d, :, ls]
                rhs = jnp.concatenate([x_pair, h_pair.astype(BF16)], axis=0)
                y0 = _dot(lhs_ref[d, 2 * hp], rhs)
                y1 = _dot(lhs_ref[d, 2 * hp + 1], rhs)
                s0 = _dot(bwt_ref[d, 2 * hp], x_pair)
                s1 = _dot(bwt_ref[d, 2 * hp + 1], x_pair)
                y_pair = jnp.where(first_half, y0, y1)
                if d == 0:
                    yf_s[pl.ds(r, rb), ls] = yf_s[pl.ds(r, rb), ls] + y_pair
                else:
                    yb_s[pl.ds(r, rb), ls] = y_pair
                cd = jnp.where(first_half[0:1, :], cd_ref[d, 2 * hp], cd_ref[d, 2 * hp + 1])
                h_s[d, :, ls] = h_pair * cd + jnp.where(first_half, s0, s1)

    stage_a = (lhs_a, bwt_a, cd_a)
    stage_b = (lhs_b, bwt_b, cd_b)
    build(0, stage_a)

    def body(j, carry):
        t = 2 * j
        build(t + 1, stage_b)
        apply(t, stage_a)
        build(t + 2, stage_a)
        apply(t + 1, stage_b)
        return carry

    n_loop = (nc - 2) // 2
    if n_loop > 0:
        lax.fori_loop(0, n_loop, body, 0)
    build(nc - 1, stage_b)
    apply(nc - 2, stage_a)
    apply(nc - 1, stage_b)

    for r0 in range(0, seq, rb):
        y = (yf_s[r0:r0 + rb, :] + yb_s[r0:r0 + rb, :]) * _silu(z_ref[r0:r0 + rb, :])
        y_ref[r0:r0 + rb, :] = _rms(y, gn_ref[0]).astype(y_ref.dtype)

    if want_state:
        for d in range(2):
            for hp in range(SSD_HEADS // 2):
                hf_ref[0, 0, d, hp * LANES:(hp + 1) * LANES, :] = h_s[d, :, hp * LANES:(hp + 1) * LANES].T


def _ssd_branch(z, xbc, dt, h0, state_buf, pw, layer, batch, seq, want_state):
    has_h0 = h0 is not None
    chain_state = state_buf is not None
    row = lambda w: pl.BlockSpec((seq, w), lambda b: (b, 0))
    state_spec = pl.BlockSpec((1, 1, 2, D_INNER, SSD_STATE), lambda b: (b, layer, 0, 0, 0))
    in_specs = [row(D_INNER), row(CONV_CH), row(DT_PAD)]
    args = [z, xbc, dt]
    if has_h0:
        in_specs.append(state_spec)
        args.append(h0)
    consts = [pw["conv_ssd_w"], pw["conv_ssd_b"], pw["dt_bias"], pw["a_log"], pw["d_skip"], pw["g_ssd_norm"]]
    in_specs += [_layer_spec(c, layer) for c in consts]
    args += consts
    aliases = {}
    if chain_state:
        aliases = {len(args): 1}
        in_specs.append(pl.BlockSpec(memory_space=pl.ANY))
        args.append(state_buf)
    out_specs = [row(D_INNER)]
    out_shape = [jax.ShapeDtypeStruct((batch * seq, D_INNER), BF16)]
    if want_state:
        out_specs.append(state_spec)
        out_shape.append(jax.ShapeDtypeStruct((batch, DEPTH, 2, D_INNER, SSD_STATE), F32))
    res = pl.pallas_call(
        functools.partial(_ssd_kernel, seq=seq, has_h0=has_h0, want_state=want_state,
                          chain_state=chain_state),
        grid=(batch,),
        in_specs=in_specs,
        out_specs=out_specs,
        out_shape=out_shape,
        input_output_aliases=aliases,
        scratch_shapes=[
            pltpu.VMEM((seq, CONV_CH - D_INNER), F32),
            pltpu.VMEM((seq, D_INNER), BF16),
            pltpu.VMEM((seq, D_INNER), F32),
            pltpu.VMEM((seq, D_INNER), F32),
            pltpu.VMEM((seq, DT_PAD), F32),
            pltpu.VMEM((seq, DT_PAD), F32),
            pltpu.VMEM((2, SSD_STATE, D_INNER), F32),
        ] + 2 * [pltpu.VMEM((2, SSD_HEADS, SSD_CHUNK, 2 * SSD_CHUNK), BF16)]
          + 2 * [pltpu.VMEM((2, SSD_HEADS, SSD_STATE, SSD_CHUNK), BF16)]
          + 2 * [pltpu.VMEM((2, SSD_HEADS, 1, SSD_STATE), F32)],
        compiler_params=_params(("parallel",)),
        name="ssd_branch",
    )(*args)
    return (res[0], res[1]) if want_state else (res[0], None)


def _mla_kernel(*refs, seq, n_cache, chain_kv):
    it = iter(refs)
    qd_ref, kvd_ref = next(it), next(it)
    latent = n_cache > 0
    qtab_ref = next(it)
    if latent:
        cckv_ref, ckpe_ref = next(it), next(it)
        kcos_ref, ksn_ref, ksp_ref = (next(it) for _ in range(3))
    gq_ref, gkv_ref, wq_ref, wk_ref, wv_ref = (next(it) for _ in range(5))
    if chain_kv:
        next(it), next(it)
    o_ref = next(it)
    if not latent:
        ckv_out_ref, kpe_out_ref = next(it), next(it)
    q_s, k_s, v_s = (next(it) for _ in range(3))
    if latent:
        o_s, s_a, s_b, m_a, m_b, p_a, p_b = (next(it) for _ in range(7))

    rb = Q_BLOCK_ROWS
    n_pairs = MLA_HEADS // 2
    lane = lax.broadcasted_iota(jnp.int32, (rb, LANES), 1)
    first_half = lane < V_DIM
    ones_even = jnp.where(lane == V_DIM, 1.0, 0.0)
    ones_odd = jnp.where(lane == 0, 1.0, 0.0)

    def pair_cols(i):
        return slice(2 * i * HEAD_SLOT, 2 * (i + 1) * HEAD_SLOT)

    for r0 in range(0, seq, rb):
        qn = _rms(qd_ref[r0:r0 + rb, :], gq_ref[0]).astype(BF16)
        tab = qtab_ref[r0:r0 + rb, :] if latent else qtab_ref[...]
        for hp in range(n_pairs):
            qq = _dot(qn, wq_ref[0, :, pair_cols(hp)])
            q_s[2 * hp, r0:r0 + rb, :] = (qq[:, :HEAD_SLOT] * tab).astype(BF16)
            q_s[2 * hp + 1, r0:r0 + rb, :] = (qq[:, HEAD_SLOT:] * tab).astype(BF16)

    def put_keys(row0, ckv_n, kpe_tile):
        kpe_slot = pltpu.roll(kpe_tile, QK_NOPE, 1) + pltpu.roll(kpe_tile, QK_NOPE + ROPE_DIM, 1)
        cb16 = ckv_n.astype(BF16)
        for hp in range(n_pairs):
            kk = _dot(cb16, wk_ref[0, :, pair_cols(hp)])
            k_s[2 * hp, row0:row0 + rb, :] = (kk[:, :HEAD_SLOT] + kpe_slot).astype(BF16)
            k_s[2 * hp + 1, row0:row0 + rb, :] = (kk[:, HEAD_SLOT:] + kpe_slot).astype(BF16)
            if latent:
                vv = _dot(cb16, wv_ref[0, :, pair_cols(hp)])
                v_s[2 * hp, row0:row0 + rb, :] = (vv[:, :HEAD_SLOT] + ones_even).astype(BF16)
                v_s[2 * hp + 1, row0:row0 + rb, :] = (vv[:, HEAD_SLOT:] + ones_odd).astype(BF16)
        if not latent:
            for i in range(n_pairs // 2):
                vv = _dot(cb16, wv_ref[0, :, pair_cols(i)])
                v_s[2 * i, row0:row0 + rb, :] = vv[:, :LANES].astype(BF16)
                v_s[2 * i + 1, row0:row0 + rb, :] = vv[:, LANES:].astype(BF16)

    for r0 in range(0, seq, rb):
        ckv_n = _rms(kvd_ref[r0:r0 + rb, 0:KV_LORA], gkv_ref[0])
        kpe = kvd_ref[r0:r0 + rb, KV_LORA:KV_PAD]
        if latent:
            kpe = (kpe * kcos_ref[r0:r0 + rb, :]
                   + pltpu.roll(kpe, LANES - 8, 1) * ksn_ref[r0:r0 + rb, :]
                   + pltpu.roll(kpe, 8, 1) * ksp_ref[r0:r0 + rb, :])
        else:
            ckv_out_ref[0, 0, r0:r0 + rb, :] = ckv_n
            kpe_out_ref[0, 0, r0:r0 + rb, :] = kpe[:, 0:ROPE_DIM]
        put_keys(r0, ckv_n, kpe)
    for r0 in range(0, n_cache, rb):
        put_keys(seq + r0, cckv_ref[0, 0, r0:r0 + rb, :], ckpe_ref[0, 0, r0:r0 + rb, :])

    def normalise(o0, o1):
        inv0 = 1.0 / o0[:, V_DIM:V_DIM + 1]
        inv1 = 1.0 / o1[:, 0:1]
        return jnp.where(first_half, o0 * inv0, o1 * inv1).astype(BF16)

    if not latent:
        for hp in range(n_pairs):
            outs = []
            for j in range(2):
                h = 2 * hp + j
                s = _dot_nt(q_s[h], k_s[h])
                p = jnp.exp2(s - jnp.max(s, axis=-1, keepdims=True))
                l = jnp.sum(p, axis=-1, keepdims=True)
                outs.append(_dot(p.astype(BF16), v_s[hp]) * (1.0 / l))
            o_ref[:, hp * LANES:(hp + 1) * LANES] = jnp.where(first_half, outs[0], outs[1]).astype(BF16)
        return

    nqb = seq // rb
    n_units = n_pairs * nqb

    def unit(u):
        if isinstance(u, int):
            return u // nqb, (u % nqb) * rb
        return u // nqb, pl.multiple_of((u % nqb) * rb, rb)

    def scores(u, s_ref, m_ref):
        hp, r = unit(u)
        for j in range(2):
            s = _dot_nt(q_s[2 * hp + j, pl.ds(r, rb), :], k_s[2 * hp + j])
            s_ref[j] = s
            m_ref[j] = jnp.max(s, axis=-1, keepdims=True)

    def probs(s_ref, m_ref, p_ref):
        for j in range(2):
            p_ref[j] = jnp.exp2(s_ref[j] - m_ref[j]).astype(BF16)

    def values(u, p_ref):
        hp, r = unit(u)
        o_s[hp, pl.ds(r, rb), :] = normalise(_dot(p_ref[0], v_s[2 * hp]), _dot(p_ref[1], v_s[2 * hp + 1]))

    scores(0, s_a, m_a)
    scores(1, s_b, m_b)
    probs(s_a, m_a, p_a)

    def body(j, carry):
        u = 2 * j
        scores(u, s_a, m_a)
        probs(s_b, m_b, p_b)
        values(u - 2, p_a)
        scores(u + 1, s_b, m_b)
        probs(s_a, m_a, p_a)
        values(u - 1, p_b)
        return carry

    lax.fori_loop(1, n_units // 2, body, 0)
    probs(s_b, m_b, p_b)
    values(n_units - 2, p_a)
    values(n_units - 1, p_b)

    for hp in range(n_pairs):
        o_ref[:, hp * LANES:(hp + 1) * LANES] = o_s[hp]


def _mla_branch(qd, kvd, cache, rope_tabs, kv_bufs, pw, layer, batch, seq):
    latent = cache is not None
    chain_kv = kv_bufs is not None
    n_cache = cache[0].shape[2] if latent else 0
    row = lambda w: pl.BlockSpec((seq, w), lambda b: (b, 0))
    in_specs = [row(Q_LORA), row(KV_PAD), _const_spec(rope_tabs[0].shape)]
    args = [qd, kvd, rope_tabs[0]]
    if latent:
        in_specs += [pl.BlockSpec((1, 1, n_cache, KV_LORA), lambda b: (b, layer, 0, 0)),
                     pl.BlockSpec((1, 1, n_cache, LANES), lambda b: (b, layer, 0, 0))]
        args += list(cache)
        in_specs += [_const_spec((seq, LANES))] * 3
        args += list(rope_tabs[1:])
    consts = [pw["g_q_norm"], pw["g_kv_norm"], pw["w_q_pad"], pw["w_k_pad"],
              pw["w_v_pad"] if latent else pw["w_v"]]
    in_specs += [_layer_spec(c, layer) for c in consts]
    args += consts
    aliases = {}
    if chain_kv:
        aliases = {len(args): 1, len(args) + 1: 2}
        in_specs += [pl.BlockSpec(memory_space=pl.ANY)] * 2
        args += list(kv_bufs)
    out_specs = [row(D_MODEL)]
    out_shape = [jax.ShapeDtypeStruct((batch * seq, D_MODEL), BF16)]
    if not latent:
        out_specs += [pl.BlockSpec((1, 1, seq, KV_LORA), lambda b: (b, layer, 0, 0)),
                      pl.BlockSpec((1, 1, seq, ROPE_DIM), lambda b: (b, layer, 0, 0))]
        out_shape += [jax.ShapeDtypeStruct((batch, DEPTH, seq, KV_LORA), F32),
                      jax.ShapeDtypeStruct((batch, DEPTH, seq, ROPE_DIM), F32)]
    n_keys = seq + n_cache
    scratch = [
        pltpu.VMEM((MLA_HEADS, seq, HEAD_SLOT), BF16),
        pltpu.VMEM((MLA_HEADS, n_keys, HEAD_SLOT), BF16),
        pltpu.VMEM((MLA_HEADS if latent else MLA_HEADS // 2, n_keys, LANES), BF16),
    ]
    if latent:
        tile = (2, Q_BLOCK_ROWS, n_keys)
        stat = (2, Q_BLOCK_ROWS, 1)
        scratch += [pltpu.VMEM((MLA_HEADS // 2, seq, LANES), BF16),
                    pltpu.VMEM(tile, F32), pltpu.VMEM(tile, F32),
                    pltpu.VMEM(stat, F32), pltpu.VMEM(stat, F32),
                    pltpu.VMEM(tile, BF16), pltpu.VMEM(tile, BF16)]
    return pl.pallas_call(
        functools.partial(_mla_kernel, seq=seq, n_cache=n_cache, chain_kv=chain_kv),
        grid=(batch,),
        in_specs=in_specs,
        out_specs=out_specs,
        out_shape=out_shape,
        input_output_aliases=aliases,
        scratch_shapes=scratch,
        compiler_params=_params(("parallel",)),
        name="mla_branch",
    )(*args)


def _fft_kernel(f_ref, cs_ref, dl_ref, o_ref, *, seq):
    scale = 1.0 / math.sqrt(seq * FFT_GROUP_W)
    for g in range(FFT_GROUPS):
        gs = slice(g * FFT_GROUP_W, (g + 1) * FFT_GROUP_W)
        t = _dot(f_ref[:, gs], cs_ref[...]).astype(BF16)
        stack = jnp.concatenate([t[:, :FFT_GROUP_W], t[:, FFT_GROUP_W:]], axis=0)
        o_ref[:, gs] = (_dot(dl_ref[...], stack) * scale).astype(o_ref.dtype)


def _fft_branch(f_in, cs, dl, batch, seq):
    row = pl.BlockSpec((seq, D_MODEL), lambda b: (b, 0))
    return pl.pallas_call(
        functools.partial(_fft_kernel, seq=seq),
        grid=(batch,),
        in_specs=[row, _const_spec(cs.shape), _const_spec(dl.shape)],
        out_specs=row,
        out_shape=jax.ShapeDtypeStruct((batch * seq, D_MODEL), BF16),
        compiler_params=_params(("parallel",)),
        name="fft_branch",
    )(f_in, cs, dl)


def _dft_tables(seq):
    def cos_sin(n):
        idx = np.arange(n, dtype=np.int64)
        ang = 2.0 * np.pi * ((idx[:, None] * idx[None, :]) % n) / n
        return np.cos(ang), np.sin(ang)

    cc, sc = cos_sin(FFT_GROUP_W)
    cl, sl = cos_sin(seq)
    cs = np.concatenate([cc, sc], axis=1).astype(np.float32)
    dl = np.concatenate([cl, -sl], axis=1).astype(np.float32)
    return jnp.asarray(cs).astype(BF16), jnp.asarray(dl).astype(BF16)


def _merge_kernel(y_ref, o_ref, f_ref, gate_ref, x_ref, g1_ref, gp_ref,
                  ws_ref, wm_ref, wf_ref, wo_ref, out_ref, *, mod_row):
    row = mod_row(pl.program_id(0))
    mix = (gate_ref[:, 0:D_MODEL] * _dot(y_ref[...], ws_ref[0])
           + gate_ref[:, D_MODEL:2 * D_MODEL] * _dot(o_ref[...], wm_ref[0])
           + gate_ref[:, 2 * D_MODEL:3 * D_MODEL] * _dot(f_ref[...], wf_ref[0]))
    out = _dot(mix.astype(BF16), wo_ref[0])
    out_ref[...] = x_ref[...] + _mod_row(g1_ref, row) * _rms(out, gp_ref[0])


def _merge(y, o, f, gates, x, mods, pw, layer, seq, latent):
    t = x.shape[0]
    row = lambda w: pl.BlockSpec((ROW_TILE, w), lambda i: (i, 0))
    consts = [pw["g_post_mix"], pw["w_ssd_out"], pw["w_mla_out"], pw["w_fft_out"], pw["w_o"]]
    return pl.pallas_call(
        functools.partial(_merge_kernel, mod_row=_row_of_tile(seq // ROW_TILE, latent)),
        grid=(t // ROW_TILE,),
        in_specs=[row(D_MODEL), row(D_MODEL), row(D_MODEL), row(N_BRANCH * D_MODEL), row(D_MODEL),
                  _mod_spec(layer, 2)] + [_layer_spec(c, layer) for c in consts],
        out_specs=row(D_MODEL),
        out_shape=jax.ShapeDtypeStruct((t, D_MODEL), F32),
        compiler_params=_params(("parallel",)),
        name="merge",
    )(y, o, f, gates, x, mods, *consts)


def _ffn_kernel(x_ref, sh_ref, sc_ref, g2_ref, gpre_ref, gpost_ref,
                wu_ref, cw_ref, cb_ref, wd_ref,
                out_ref, h_s, acc_s, u_a, u_b, *, seq, mod_row):
    row = mod_row(pl.program_id(0))
    h_s[...] = (_rms(x_ref[...], gpre_ref[0]) * (1.0 + _mod_row(sc_ref, row))
                + _mod_row(sh_ref, row)).astype(BF16)

    rid = lax.broadcasted_iota(jnp.int32, (FFN_ROWS, FFN_BLOCK), 0) & (seq - 1)
    is_first = rid == 0
    is_last = rid == seq - 1

    def cols(k, gate_half):
        c0 = (D_FF if gate_half else 0) + k * FFN_BLOCK
        if isinstance(k, int):
            return slice(c0, c0 + FFN_BLOCK)
        return pl.ds(pl.multiple_of(c0, FFN_BLOCK), FFN_BLOCK)

    def rows(k):
        if isinstance(k, int):
            return slice(k * FFN_BLOCK, (k + 1) * FFN_BLOCK)
        return pl.ds(pl.multiple_of(k * FFN_BLOCK, FFN_BLOCK), FFN_BLOCK)

    def up(k, u_ref):
        h = h_s[...]
        u_ref[0] = _dot(h, wu_ref[0, :, cols(k, False)])
        u_ref[1] = _dot(h, wu_ref[0, :, cols(k, True)])

    def conv(u, cs):
        prev = jnp.where(is_first, 0.0, pltpu.roll(u, 1, 0))
        nxt = jnp.where(is_last, 0.0, pltpu.roll(u, FFN_ROWS - 1, 0))
        return (prev * cw_ref[0, 0:1, cs] + u * cw_ref[0, 1:2, cs] + nxt * cw_ref[0, 2:3, cs]
                + cb_ref[0, :, cs])

    def down(k, u_ref, first=False):
        a = conv(u_ref[0], cols(k, False))
        g = conv(u_ref[1], cols(k, True))
        d = _dot((_silu(a) * g).astype(BF16), wd_ref[0, rows(k), :])
        if first:
            acc_s[...] = d
        else:
            acc_s[...] += d

    up(0, u_a)
    up(1, u_b)
    down(0, u_a, first=True)

    def body(i, carry):
        up(2 * i, u_a)
        down(2 * i - 1, u_b)
        up(2 * i + 1, u_b)
        down(2 * i, u_a)
        return carry

    n_pairs = (FFN_NBLK - 1) // 2
    lax.fori_loop(1, n_pairs, body, 0)
    last = 2 * n_pairs
    up(last, u_a)
    down(last - 1, u_b)
    down(last, u_a)

    out_ref[...] = x_ref[...] + _mod_row(g2_ref, row) * _rms(acc_s[...], gpost_ref[0])


def _ffn(x, mods, pw, layer, seq, latent):
    t = x.shape[0]
    row = pl.BlockSpec((FFN_ROWS, D_MODEL), lambda i: (i, 0))
    consts = [pw["g_pre_ffn"], pw["g_post_ffn"], pw["w_up"], pw["conv_ffn_w"], pw["conv_ffn_b"], pw["w_down"]]
    stage = pltpu.VMEM((2, FFN_ROWS, FFN_BLOCK), F32)
    return pl.pallas_call(
        functools.partial(_ffn_kernel, seq=seq, mod_row=_row_of_tile(max(seq // FFN_ROWS, 1), latent)),
        grid=(t // FFN_ROWS,),
        in_specs=[row, _mod_spec(layer, 3), _mod_spec(layer, 4), _mod_spec(layer, 5)]
        + [_layer_spec(c, layer) for c in consts],
        out_specs=row,
        out_shape=jax.ShapeDtypeStruct((t, D_MODEL), F32),
        scratch_shapes=[pltpu.VMEM((FFN_ROWS, D_MODEL), BF16), pltpu.VMEM((FFN_ROWS, D_MODEL), F32),
                        stage, stage],
        compiler_params=_params(("parallel",)),
        name="conv_mlp",
    )(x, mods, mods, mods, *consts)


def _pack_tail(w_in):
    zeros = lambda n: jnp.zeros(w_in.shape[:2] + (n,), w_in.dtype)
    parts = [
        w_in[:, :, OFF_DT:OFF_QD], zeros(DT_PAD - 2 * SSD_HEADS),
        w_in[:, :, OFF_QD:OFF_KVD],
        w_in[:, :, OFF_KVD:OFF_FFT], zeros(KV_PAD - KV_LORA - ROPE_DIM),
        w_in[:, :, OFF_FFT:],
    ]
    return jnp.concatenate(parts, axis=2).astype(BF16)


def _rope_partner():
    idx = np.arange(ROPE_DIM)
    quarter = ROPE_DIM // 4
    return np.where(idx % (2 * quarter) < quarter, idx + quarter, idx - quarter)


def _pad_q_up(w_q_up):
    w = w_q_up.reshape(DEPTH, Q_LORA, MLA_HEADS, QK_NOPE + ROPE_DIM)
    pe = w[..., QK_NOPE:]
    w = jnp.concatenate([w, pe[..., _rope_partner()]], axis=-1)
    return w.reshape(DEPTH, Q_LORA, MLA_HEADS * HEAD_SLOT).astype(BF16)


def _pad_kv_up(w_kv_up):
    w = w_kv_up.reshape(DEPTH, KV_LORA, MLA_HEADS // 2, 2, QK_NOPE + V_DIM)
    zero = jnp.zeros_like(w[..., :QK_NOPE])
    wk = jnp.concatenate([w[..., :QK_NOPE], zero], axis=-1)
    v = w[..., QK_NOPE:]
    wv = jnp.stack([jnp.concatenate([v[..., 0, :], zero[..., 0, :]], axis=-1),
                    jnp.concatenate([zero[..., 1, :], v[..., 1, :]], axis=-1)], axis=3)
    shape = (DEPTH, KV_LORA, MLA_HEADS * HEAD_SLOT)
    return wk.reshape(shape).astype(BF16), wv.reshape(shape).astype(BF16)


def _rope_tables(seq):
    rows = seq // GRID_W
    row = jnp.repeat(jnp.arange(rows), GRID_W)
    col = jnp.tile(jnp.arange(GRID_W), rows)
    quarter = ROPE_DIM // 4
    inv = ROPE_THETA ** (-jnp.arange(quarter, dtype=F32) / quarter)
    ang_row = row.astype(F32)[:, None] * inv
    ang_col = col.astype(F32)[:, None] * inv
    cos = jnp.concatenate([jnp.cos(ang_row)] * 2 + [jnp.cos(ang_col)] * 2, axis=1)
    sin = jnp.concatenate([jnp.sin(ang_row)] * 2 + [jnp.sin(ang_col)] * 2, axis=1)
    lo = (jnp.arange(ROPE_DIM) % (2 * quarter)) < quarter
    sin_signed = jnp.where(lo, -sin, sin)
    scale = ATTN_SCALE * LOG2E
    q_tab = jnp.concatenate([jnp.full((seq, QK_NOPE), scale, F32), scale * cos, scale * sin_signed], axis=1)
    pad = lambda t, fill: jnp.concatenate([t, jnp.full((seq, LANES - ROPE_DIM), fill, F32)], axis=1)
    return (q_tab, pad(cos, 1.0), pad(jnp.where(lo, -sin, 0.0), 0.0), pad(jnp.where(lo, 0.0, sin), 0.0))


def _ctx_q_table():
    scale = ATTN_SCALE * LOG2E
    return jnp.concatenate([jnp.full((1, QK_NOPE + ROPE_DIM), scale, F32),
                            jnp.zeros((1, HEAD_SLOT - QK_NOPE - ROPE_DIM), F32)], axis=1)


def _trunk_pass(x, mods, pw, layer, ctx, outs, tabs, batch, seq):
    latent = ctx is not None
    z, xbc, dt, qd, kvd, f_in, gates = _pre_mix(x, mods, pw, layer, seq, latent)
    y_ssd, ssm = _ssd_branch(z, xbc, dt, ctx[2] if latent else None, outs[2] if outs else None,
                             pw, layer, batch, seq, want_state=not latent)
    mla = _mla_branch(qd, kvd, (ctx[0], ctx[1]) if latent else None, tabs["rope"],
                      (outs[0], outs[1]) if outs else None, pw, layer, batch, seq)
    f = _fft_branch(f_in, tabs["cs"], tabs["dl"], batch, seq)
    x1 = _merge(y_ssd, mla[0], f, gates, x, mods, pw, layer, seq, latent)
    x2 = _ffn(x1, mods, pw, layer, seq, latent)
    if latent:
        return x2, None
    return x2, (mla[1], mla[2], ssm)


def kernel(x_prompt, x_sample, cache_ckv, cache_kpe, state_ssm, c, c_ctx, w_ada, b_ada, g_pre_mix, g_post_mix, g_pre_ffn, g_post_ffn, w_in, w_gate, b_gate, w_o, conv_ssd_w, conv_ssd_b, dt_bias, a_log, d_skip, g_ssd_norm, w_ssd_out, g_q_norm, w_q_up, g_kv_norm, w_kv_up, w_mla_out, w_fft_out, w_up, conv_ffn_w, conv_ffn_b, w_down):
    batch, seq, _ = x_prompt.shape
    dec_batch, dec_seq, _ = x_sample.shape

    cvec = jnp.concatenate([c_ctx[None, :], c, jnp.zeros((SUBLANES - 1 - dec_batch, D_MODEL), F32)], axis=0)
    mods = _ada_params(cvec, w_ada, b_ada)

    vec = lambda v: v[:, None, :]
    pad_lane = lambda v: jnp.pad(v, ((0, 0), (0, 0), (0, LANES - v.shape[-1])))
    w_k_pad, w_v_pad = _pad_kv_up(w_kv_up)
    pw = {
        "g_pre_mix": vec(g_pre_mix), "g_post_mix": vec(g_post_mix),
        "g_pre_ffn": vec(g_pre_ffn), "g_post_ffn": vec(g_post_ffn),
        "w_head": w_in[:, :, :HEAD_COLS].astype(BF16), "w_tail": _pack_tail(w_in),
        "w_gate": w_gate.astype(BF16), "b_gate": vec(b_gate),
        "w_o": w_o.astype(BF16), "w_ssd_out": w_ssd_out.astype(BF16),
        "w_mla_out": w_mla_out.astype(BF16), "w_fft_out": w_fft_out.astype(BF16),
        "conv_ssd_w": conv_ssd_w, "conv_ssd_b": vec(conv_ssd_b),
        "dt_bias": pad_lane(dt_bias.reshape(DEPTH, 1, 2 * SSD_HEADS)),
        "a_log": pad_lane(a_log.reshape(DEPTH, 1, 2 * SSD_HEADS)),
        "d_skip": vec(jnp.repeat(d_skip, SSD_HEAD_DIM, axis=1)),
        "g_ssd_norm": vec(g_ssd_norm),
        "g_q_norm": vec(g_q_norm), "g_kv_norm": vec(g_kv_norm),
        "w_q_pad": _pad_q_up(w_q_up), "w_k_pad": w_k_pad, "w_v_pad": w_v_pad,
        "w_v": w_kv_up.reshape(DEPTH, KV_LORA, MLA_HEADS, QK_NOPE + V_DIM)[..., QK_NOPE:]
        .reshape(DEPTH, KV_LORA, MLA_HEADS * V_DIM).astype(BF16),
        "w_up": w_up.astype(BF16), "conv_ffn_w": conv_ffn_w, "conv_ffn_b": vec(conv_ffn_b),
        "w_down": w_down.astype(BF16),
    }

    cs, dl_ctx = _dft_tables(seq)
    _, dl_lat = _dft_tables(dec_seq)
    tabs_ctx = {"rope": (_ctx_q_table(),), "cs": cs, "dl": dl_ctx}
    tabs_lat = {"rope": _rope_tables(dec_seq), "cs": cs, "dl": dl_lat}
    cache_kpe_p = jnp.pad(cache_kpe, ((0, 0), (0, 0), (0, 0), (0, LANES - ROPE_DIM)))
    h0_all = state_ssm.reshape(dec_batch, DEPTH, 2, D_INNER, SSD_STATE)
    ctx = (cache_ckv, cache_kpe_p, h0_all)

    y_p = x_prompt.reshape(batch * seq, D_MODEL)
    y_s = x_sample.reshape(dec_batch * dec_seq, D_MODEL)
    outs = None
    for i in range(DEPTH):
        y_p, outs = _trunk_pass(y_p, mods, pw, i, None, outs, tabs_ctx, batch, seq)
        y_s, _ = _trunk_pass(y_s, mods, pw, i, ctx, None, tabs_lat, dec_batch, dec_seq)
    new_ckv, new_kpe, new_ssm = outs
    return (y_p.reshape(batch, seq, D_MODEL), y_s.reshape(dec_batch, dec_seq, D_MODEL),
            new_ckv, new_kpe, new_ssm.reshape(batch, DEPTH, 2, SSD_HEADS, SSD_HEAD_DIM, SSD_STATE))
```

```python
import functools
import math

import numpy as np
import jax
import jax.numpy as jnp
from jax import lax
from jax.experimental import pallas as pl
from jax.experimental.pallas import tpu as pltpu

F32 = jnp.float32
BF16 = jnp.bfloat16

D_MODEL = 1024
DEPTH = 4
GRID_W = 64
EPS = 1e-6
SSD_HEADS = 16
SSD_HEAD_DIM = 64
D_INNER = SSD_HEADS * SSD_HEAD_DIM
SSD_GROUPS = 2
SSD_STATE = 128
SSD_CHUNK = 128
CONV_CH = D_INNER + 2 * SSD_GROUPS * SSD_STATE
MLA_HEADS = 16
Q_LORA = 384
KV_LORA = 256
QK_NOPE = 64
ROPE_DIM = 32
V_DIM = 64
ROPE_THETA = 10000.0
ATTN_SCALE = (QK_NOPE + ROPE_DIM) ** -0.5
FFT_GROUPS = 4
FFT_GROUP_W = 256
D_FF = 2816
N_BRANCH = 3
N_MOD = 6
OFF_XBC = D_INNER
OFF_DT = OFF_XBC + CONV_CH
OFF_QD = OFF_DT + 2 * SSD_HEADS
OFF_KVD = OFF_QD + Q_LORA
OFF_FFT = OFF_KVD + KV_LORA + ROPE_DIM

LANES = 128
SUBLANES = 8
HEAD_SLOT = LANES
KV_PAD = KV_LORA + LANES
DT_PAD = LANES

ROW_TILE = 256
FFN_ROWS = 1024
FFN_BLOCK = 256
FFN_NBLK = D_FF // FFN_BLOCK
Q_BLOCK_ROWS = 256
VMEM_LIMIT = 56 * 1024 * 1024
NEG_BIG = -1e30
LOG2E = math.log2(math.e)


def _params(sem, vmem=VMEM_LIMIT):
    return pltpu.CompilerParams(dimension_semantics=sem, vmem_limit_bytes=vmem)


def _sigmoid(x):
    return 1.0 / (1.0 + jnp.exp(-x))


def _silu(x):
    return x * _sigmoid(x)


def _rms(x, g):
    return x * lax.rsqrt(jnp.mean(x * x, axis=-1, keepdims=True) + EPS) * g


def _dot(a, b):
    return jnp.dot(a, b, preferred_element_type=F32)


def _dot_nt(a, b):
    return lax.dot_general(a, b, (((1,), (1,)), ((), ())), preferred_element_type=F32)


def _const_spec(shape):
    nd = len(shape)
    return pl.BlockSpec(shape, lambda *_: (0,) * nd, pipeline_mode=pl.Buffered(1))


def _layer_spec(arr, layer):
    shape = (1,) + tuple(arr.shape[1:])
    nd = len(shape)
    return pl.BlockSpec(shape, lambda *_: (layer,) + (0,) * (nd - 1), pipeline_mode=pl.Buffered(1))


def _mod_spec(layer, k):
    return pl.BlockSpec((1, 1, SUBLANES, D_MODEL), lambda *_: (layer, k, 0, 0), pipeline_mode=pl.Buffered(1))


def _mod_row(ref, row):
    return ref[0, 0, pl.ds(row, 1), :]


def _ada_kernel(c_ref, w_ref, b_ref, o_ref):
    c = c_ref[...]
    s = _silu(c).astype(BF16)
    o_ref[0, 0] = _dot(s, w_ref[0].astype(BF16)) + b_ref[0]


def _ada_params(cvec, w_ada, b_ada):
    return pl.pallas_call(
        _ada_kernel,
        grid=(DEPTH, N_MOD),
        in_specs=[
            pl.BlockSpec((SUBLANES, D_MODEL), lambda l, j: (0, 0)),
            pl.BlockSpec((1, D_MODEL, D_MODEL), lambda l, j: (l, 0, j)),
            pl.BlockSpec((1, 1, D_MODEL), lambda l, j: (l, 0, j)),
        ],
        out_specs=pl.BlockSpec((1, 1, SUBLANES, D_MODEL), lambda l, j: (l, j, 0, 0)),
        out_shape=jax.ShapeDtypeStruct((DEPTH, N_MOD, SUBLANES, D_MODEL), F32),
        compiler_params=_params(("parallel", "parallel")),
        name="ada_params",
    )(cvec, w_ada, b_ada.reshape(DEPTH, 1, N_MOD * D_MODEL))


def _pre_kernel(x_ref, sh_ref, sc_ref, g_ref, wi_ref, wg_ref, bg_ref,
                z_ref, xbc_ref, dt_ref, qd_ref, kvd_ref, fin_ref, gate_ref, *, mod_row):
    row = mod_row(pl.program_id(0))
    x = x_ref[...]
    h = (_rms(x, g_ref[0]) * (1.0 + _mod_row(sc_ref, row)) + _mod_row(sh_ref, row)).astype(BF16)

    def seg(out_ref, w_ref, start, width, bias_ref=None):
        step = 512
        for c0 in range(0, width, step):
            w = min(step, width - c0)
            r = _dot(h, w_ref[0, :, start + c0:start + c0 + w])
            if bias_ref is not None:
                r = _sigmoid(r + bias_ref[0, :, c0:c0 + w])
            out_ref[:, c0:c0 + w] = r.astype(out_ref.dtype)

    seg(z_ref, wi_ref, 0, D_INNER)
    seg(xbc_ref, wi_ref, OFF_XBC, CONV_CH)
    t = _dot(h, wi_ref[0, :, OFF_DT:])
    dt_ref[...] = t[:, 0:DT_PAD]
    qd_ref[...] = t[:, OFF_QD - OFF_DT:OFF_KVD - OFF_DT]
    kvd_ref[...] = t[:, OFF_KVD - OFF_DT:OFF_KVD - OFF_DT + KV_PAD]
    fin_ref[...] = t[:, OFF_FFT - OFF_DT:].astype(fin_ref.dtype)
    seg(gate_ref, wg_ref, 0, N_BRANCH * D_MODEL, bg_ref)


def _row_of_tile(rows_per_batch, latent):
    if latent:
        return lambda i: 1 + i // rows_per_batch
    return lambda i: 0


def _pre_mix(x, mods, pw, layer, seq, latent):
    t = x.shape[0]

    def row_spec(width):
        return pl.BlockSpec((ROW_TILE, width), lambda i: (i, 0))

    widths = (D_INNER, CONV_CH, DT_PAD, Q_LORA, KV_PAD, D_MODEL, N_BRANCH * D_MODEL)
    dtypes = (F32, F32, F32, F32, F32, BF16, F32)
    consts = [pw["g_pre_mix"], pw["w_in"], pw["w_gate"], pw["b_gate"]]
    return pl.pallas_call(
        functools.partial(_pre_kernel, mod_row=_row_of_tile(seq // ROW_TILE, latent)),
        grid=(t // ROW_TILE,),
        in_specs=[row_spec(D_MODEL), _mod_spec(layer, 0), _mod_spec(layer, 1)]
        + [_layer_spec(c, layer) for c in consts],
        out_specs=[row_spec(w) for w in widths],
        out_shape=[jax.ShapeDtypeStruct((t, w), d) for w, d in zip(widths, dtypes)],
        compiler_params=_params(("parallel",)),
        name="pre_mix",
    )(x, mods, mods, *consts)


def _split3(x):
    hi = x.astype(BF16)
    r1 = x - hi.astype(F32)
    mid = r1.astype(BF16)
    lo = (r1 - mid.astype(F32)).astype(BF16)
    return hi, mid, lo


def _ssd_kernel(*refs, seq, has_h0, want_state, chain_state, layer):
    it = iter(refs)
    z_ref, xbc_ref, dt_ref = next(it), next(it), next(it)
    h0_ref = next(it) if has_h0 else None
    cw_ref, cb_ref, dtb_ref, alog_ref, dsk_ref, gn_ref = (next(it) for _ in range(6))
    if chain_state:
        next(it)
    y_ref = next(it)
    hf_ref = next(it) if want_state else None
    bc_s, xs_s, yf_s, yb_s, ld_s, da_s, h_s = (next(it) for _ in range(7))
    lhs_a, lhs_b, bwt_a, bwt_b, cd_a, cd_b = (next(it) for _ in range(6))

    nc = seq // SSD_CHUNK
    rb = SSD_CHUNK

    cstep = 512
    for r0 in range(0, seq, rb):
        for c0 in range(0, CONV_CH, cstep):
            cs = slice(c0, c0 + cstep)
            cur = xbc_ref[r0:r0 + rb, cs]
            rid = lax.broadcasted_iota(jnp.int32, (rb, cstep), 0)
            prev = pltpu.roll(cur, 1, 0)
            if r0 > 0:
                prev = jnp.where(rid == 0, xbc_ref[r0 - 1:r0, cs], prev)
            else:
                prev = jnp.where(rid == 0, 0.0, prev)
            nxt = pltpu.roll(cur, rb - 1, 0)
            if r0 + rb < seq:
                nxt = jnp.where(rid == rb - 1, xbc_ref[r0 + rb:r0 + rb + 1, cs], nxt)
            else:
                nxt = jnp.where(rid == rb - 1, 0.0, nxt)
            u = (prev * cw_ref[0, 0:1, cs] + cur * cw_ref[0, 1:2, cs] + nxt * cw_ref[0, 2:3, cs]
                 + cb_ref[0, :, cs])
            u = _silu(u)
            if c0 < D_INNER:
                xs_s[r0:r0 + rb, cs] = u.astype(BF16)
                yf_s[r0:r0 + rb, cs] = u * dsk_ref[0, :, cs]
            else:
                bc_s[r0:r0 + rb, c0 - D_INNER:c0 - D_INNER + cstep] = u

    a_row = -jnp.exp(alog_ref[0]) * LOG2E
    for r0 in range(0, seq, rb):
        v = dt_ref[r0:r0 + rb, :] + dtb_ref[0]
        dtv = jnp.maximum(v, 0.0) + jnp.log1p(jnp.exp(-jnp.abs(v)))
        ld_s[r0:r0 + rb, :] = jnp.log2(dtv)
        da_s[r0:r0 + rb, :] = dtv * a_row

    for d in range(2):
        if has_h0:
            for hp in range(SSD_HEADS // 2):
                blk = h0_ref[0, 0, d, hp * LANES:(hp + 1) * LANES, :]
                h_s[d, :, hp * LANES:(hp + 1) * LANES] = blk.T
        else:
            h_s[d] = jnp.zeros((SSD_STATE, D_INNER), F32)

    ri = lax.broadcasted_iota(jnp.int32, (rb, rb), 0)
    ci = lax.broadcasted_iota(jnp.int32, (rb, rb), 1)
    lower = ri >= ci
    upper = ri <= ci
    tri_f = jnp.where(lower, 1.0, 0.0).astype(BF16)
    tri_b = jnp.where(upper, 1.0, 0.0).astype(BF16)
    first_half = ci < SSD_HEAD_DIM

    def rows_of(step, d):
        c = step if d == 0 else nc - 1 - step
        return c * rb if isinstance(c, int) else pl.multiple_of(c * rb, rb)

    def build(step, stage):
        lhs_ref, bwt_ref, cd_ref = stage
        for d in range(2):
            r = rows_of(step, d)
            mask = lower if d == 0 else upper
            tri = tri_f if d == 0 else tri_b
            end = rb - 1 if d == 0 else 0
            hi, mid, lo = _split3(da_s[pl.ds(r, rb), :])
            acum = _dot(tri, hi) + _dot(tri, mid) + _dot(tri, lo)
            acum_t = acum.T
            rl_t = acum_t - ld_s[pl.ds(r, rb), :].T
            w_t = jnp.exp2(acum_t[:, end:end + 1] - rl_t)
            for g in range(SSD_GROUPS):
                b_g = bc_s[pl.ds(r, rb), g * SSD_STATE:(g + 1) * SSD_STATE]
                c_g = bc_s[pl.ds(r, rb), (SSD_GROUPS + g) * SSD_STATE:(SSD_GROUPS + g + 1) * SSD_STATE]
                cb = _dot_nt(c_g.astype(BF16), b_g.astype(BF16))
                b_t = b_g.T
                for e in range(g * 8, g * 8 + 8):
                    k = d * SSD_HEADS + e
                    colb = jnp.broadcast_to(acum[:, k:k + 1], (rb, rb))
                    m_e = jnp.exp2(jnp.where(mask, colb - rl_t[k:k + 1, :], NEG_BIG)) * cb
                    e_col = jnp.exp2(colb)
                    lhs_ref[d, e] = jnp.concatenate([m_e.astype(BF16), (c_g * e_col).astype(BF16)], axis=1)
                    bwt_ref[d, e] = (b_t * w_t[k:k + 1, :]).astype(BF16)
                    cd_ref[d, e] = e_col[end:end + 1, :]

    def apply(step, stage):
        lhs_ref, bwt_ref, cd_ref = stage
        for d in range(2):
            r = rows_of(step, d)
            for hp in range(SSD_HEADS // 2):
                ls = slice(hp * LANES, (hp + 1) * LANES)
                x_pair = xs_s[pl.ds(r, rb), ls]
                h_pair = h_s[d, :, ls]
                rhs = jnp.concatenate([x_pair, h_pair.astype(BF16)], axis=0)
                y0 = _dot(lhs_ref[d, 2 * hp], rhs)
                y1 = _dot(lhs_ref[d, 2 * hp + 1], rhs)
                s0 = _dot(bwt_ref[d, 2 * hp], x_pair)
                s1 = _dot(bwt_ref[d, 2 * hp + 1], x_pair)
                y_pair = jnp.where(first_half, y0, y1)
                if d == 0:
                    yf_s[pl.ds(r, rb), ls] = yf_s[pl.ds(r, rb), ls] + y_pair
                else:
                    yb_s[pl.ds(r, rb), ls] = y_pair
                cd = jnp.where(first_half[0:1, :], cd_ref[d, 2 * hp], cd_ref[d, 2 * hp + 1])
                h_s[d, :, ls] = h_pair * cd + jnp.where(first_half, s0, s1)

    stage_a = (lhs_a, bwt_a, cd_a)
    stage_b = (lhs_b, bwt_b, cd_b)
    build(0, stage_a)

    def body(j, carry):
        t = 2 * j
        build(t + 1, stage_b)
        apply(t, stage_a)
        build(t + 2, stage_a)
        apply(t + 1, stage_b)
        return carry

    n_loop = (nc - 2) // 2
    if n_loop > 0:
        lax.fori_loop(0, n_loop, body, 0)
    build(nc - 1, stage_b)
    apply(nc - 2, stage_a)
    apply(nc - 1, stage_b)

    for r0 in range(0, seq, rb):
        y = (yf_s[r0:r0 + rb, :] + yb_s[r0:r0 + rb, :]) * _silu(z_ref[r0:r0 + rb, :])
        y_ref[r0:r0 + rb, :] = _rms(y, gn_ref[0]).astype(y_ref.dtype)

    if want_state:
        for d in range(2):
            for hp in range(SSD_HEADS // 2):
                slot = 0 if chain_state else layer
                hf_ref[0, slot, d, hp * LANES:(hp + 1) * LANES, :] = h_s[d, :, hp * LANES:(hp + 1) * LANES].T
        if not chain_state:
            for other in range(DEPTH):
                if other != layer:
                    hf_ref[0, other] = jnp.zeros((2, D_INNER, SSD_STATE), F32)


def _ssd_branch(z, xbc, dt, h0, state_buf, pw, layer, batch, seq, want_state):
    has_h0 = h0 is not None
    chain_state = state_buf is not None
    row = lambda w: pl.BlockSpec((seq, w), lambda b: (b, 0))
    state_spec = pl.BlockSpec((1, 1, 2, D_INNER, SSD_STATE), lambda b: (b, layer, 0, 0, 0))
    in_specs = [row(D_INNER), row(CONV_CH), row(DT_PAD)]
    args = [z, xbc, dt]
    if has_h0:
        in_specs.append(state_spec)
        args.append(h0)
    consts = [pw["conv_ssd_w"], pw["conv_ssd_b"], pw["dt_bias"], pw["a_log"], pw["d_skip"], pw["g_ssd_norm"]]
    in_specs += [_layer_spec(c, layer) for c in consts]
    args += consts
    aliases = {}
    if chain_state:
        aliases = {len(args): 1}
        in_specs.append(pl.BlockSpec(memory_space=pl.ANY))
        args.append(state_buf)
    out_specs = [row(D_INNER)]
    out_shape = [jax.ShapeDtypeStruct((batch * seq, D_INNER), BF16)]
    if want_state:
        out_specs.append(state_spec if chain_state else
                         pl.BlockSpec((1, DEPTH, 2, D_INNER, SSD_STATE), lambda b: (b, 0, 0, 0, 0)))
        out_shape.append(jax.ShapeDtypeStruct((batch, DEPTH, 2, D_INNER, SSD_STATE), F32))
    res = pl.pallas_call(
        functools.partial(_ssd_kernel, seq=seq, has_h0=has_h0, want_state=want_state,
                          chain_state=chain_state, layer=layer),
        grid=(batch,),
        in_specs=in_specs,
        out_specs=out_specs,
        out_shape=out_shape,
        input_output_aliases=aliases,
        scratch_shapes=[
            pltpu.VMEM((seq, CONV_CH - D_INNER), F32),
            pltpu.VMEM((seq, D_INNER), BF16),
            pltpu.VMEM((seq, D_INNER), F32),
            pltpu.VMEM((seq, D_INNER), F32),
            pltpu.VMEM((seq, DT_PAD), F32),
            pltpu.VMEM((seq, DT_PAD), F32),
            pltpu.VMEM((2, SSD_STATE, D_INNER), F32),
        ] + 2 * [pltpu.VMEM((2, SSD_HEADS, SSD_CHUNK, 2 * SSD_CHUNK), BF16)]
          + 2 * [pltpu.VMEM((2, SSD_HEADS, SSD_STATE, SSD_CHUNK), BF16)]
          + 2 * [pltpu.VMEM((2, SSD_HEADS, 1, SSD_STATE), F32)],
        compiler_params=_params(("parallel",)),
        name="ssd_branch",
    )(*args)
    return (res[0], res[1]) if want_state else (res[0], None)


def _mla_kernel(*refs, seq, n_cache, chain_kv, layer):
    it = iter(refs)
    qd_ref, kvd_ref = next(it), next(it)
    latent = n_cache > 0
    qtab_ref = next(it)
    if latent:
        cckv_ref, ckpe_ref = next(it), next(it)
        kcos_ref, ksn_ref, ksp_ref = (next(it) for _ in range(3))
    gq_ref, gkv_ref, wq_ref, wkv_ref = (next(it) for _ in range(4))
    if chain_kv:
        next(it), next(it)
    o_ref = next(it)
    if not latent:
        ckv_out_ref, kpe_out_ref = next(it), next(it)
    q_s, k_s, v_s = (next(it) for _ in range(3))
    if latent:
        o_s, s_a, s_b, m_a, m_b, p_a, p_b = (next(it) for _ in range(7))

    rb = Q_BLOCK_ROWS
    n_pairs = MLA_HEADS // 2
    lane = lax.broadcasted_iota(jnp.int32, (rb, LANES), 1)
    first_half = lane < V_DIM
    ones_even = jnp.where(lane == V_DIM, 1.0, 0.0)
    ones_odd = jnp.where(lane == 0, 1.0, 0.0)

    def pair_cols(i):
        return slice(2 * i * HEAD_SLOT, 2 * (i + 1) * HEAD_SLOT)

    for r0 in range(0, seq, rb):
        qn = _rms(qd_ref[r0:r0 + rb, :], gq_ref[0]).astype(BF16)
        tab = qtab_ref[r0:r0 + rb, :] if latent else qtab_ref[...]
        for hp in range(n_pairs):
            qq = _dot(qn, wq_ref[0, :, pair_cols(hp)])
            q_s[2 * hp, r0:r0 + rb, :] = (qq[:, :HEAD_SLOT] * tab).astype(BF16)
            q_s[2 * hp + 1, r0:r0 + rb, :] = (qq[:, HEAD_SLOT:] * tab).astype(BF16)

    def put_keys(row0, ckv_n, kpe_tile):
        kpe_slot = pltpu.roll(kpe_tile, QK_NOPE, 1) + pltpu.roll(kpe_tile, QK_NOPE + ROPE_DIM, 1)
        cb16 = ckv_n.astype(BF16)
        for hp in range(n_pairs):
            kv = _dot(cb16, wkv_ref[0, :, pair_cols(hp)])
            kv_e, kv_o = kv[:, :HEAD_SLOT], kv[:, HEAD_SLOT:]
            k_s[2 * hp, row0:row0 + rb, :] = (jnp.where(first_half, kv_e, 0.0) + kpe_slot).astype(BF16)
            k_s[2 * hp + 1, row0:row0 + rb, :] = (jnp.where(first_half, kv_o, 0.0) + kpe_slot).astype(BF16)
            v_e = pltpu.roll(kv_e, V_DIM, 1)
            if latent:
                v_s[2 * hp, row0:row0 + rb, :] = (jnp.where(first_half, v_e, 0.0) + ones_even).astype(BF16)
                v_s[2 * hp + 1, row0:row0 + rb, :] = (jnp.where(first_half, 0.0, kv_o) + ones_odd).astype(BF16)
            else:
                v_s[hp, row0:row0 + rb, :] = jnp.where(first_half, v_e, kv_o).astype(BF16)

    for r0 in range(0, seq, rb):
        ckv_n = _rms(kvd_ref[r0:r0 + rb, 0:KV_LORA], gkv_ref[0])
        kpe = jnp.where(lane < ROPE_DIM, kvd_ref[r0:r0 + rb, KV_LORA:KV_PAD], 0.0)
        if latent:
            kpe = (kpe * kcos_ref[r0:r0 + rb, :]
                   + pltpu.roll(kpe, LANES - 8, 1) * ksn_ref[r0:r0 + rb, :]
                   + pltpu.roll(kpe, 8, 1) * ksp_ref[r0:r0 + rb, :])
        else:
            slot = 0 if chain_kv else layer
            ckv_out_ref[0, slot, r0:r0 + rb, :] = ckv_n
            kpe_out_ref[0, slot, r0:r0 + rb, :] = kpe[:, 0:ROPE_DIM]
            if not chain_kv:
                for other in range(DEPTH):
                    if other != layer:
                        ckv_out_ref[0, other, r0:r0 + rb, :] = jnp.zeros((rb, KV_LORA), F32)
                        kpe_out_ref[0, other, r0:r0 + rb, :] = jnp.zeros((rb, ROPE_DIM), F32)
        put_keys(r0, ckv_n, kpe)
    for r0 in range(0, n_cache, rb):
        put_keys(seq + r0, cckv_ref[0, 0, r0:r0 + rb, :], ckpe_ref[0, 0, r0:r0 + rb, :])

    def normalise(o0, o1):
        inv0 = 1.0 / o0[:, V_DIM:V_DIM + 1]
        inv1 = 1.0 / o1[:, 0:1]
        return jnp.where(first_half, o0 * inv0, o1 * inv1).astype(BF16)

    if not latent:
        for hp in range(n_pairs):
            outs = []
            for j in range(2):
                h = 2 * hp + j
                s = _dot_nt(q_s[h], k_s[h])
                p = jnp.exp2(s - jnp.max(s, axis=-1, keepdims=True))
                l = jnp.sum(p, axis=-1, keepdims=True)
                outs.append(_dot(p.astype(BF16), v_s[hp]) * (1.0 / l))
            o_ref[:, hp * LANES:(hp + 1) * LANES] = jnp.where(first_half, outs[0], outs[1]).astype(BF16)
        return

    nqb = seq // rb
    n_units = n_pairs * nqb

    def unit(u):
        if isinstance(u, int):
            return u // nqb, (u % nqb) * rb
        return u // nqb, pl.multiple_of((u % nqb) * rb, rb)

    def scores(u, s_ref, m_ref):
        hp, r = unit(u)
        for j in range(2):
            s = _dot_nt(q_s[2 * hp + j, pl.ds(r, rb), :], k_s[2 * hp + j])
            s_ref[j] = s
            m_ref[j] = jnp.max(s, axis=-1, keepdims=True)

    def probs(s_ref, m_ref, p_ref):
        for j in range(2):
            p_ref[j] = jnp.exp2(s_ref[j] - m_ref[j]).astype(BF16)

    def values(u, p_ref):
        hp, r = unit(u)
        o_s[hp, pl.ds(r, rb), :] = normalise(_dot(p_ref[0], v_s[2 * hp]), _dot(p_ref[1], v_s[2 * hp + 1]))

    scores(0, s_a, m_a)
    scores(1, s_b, m_b)
    probs(s_a, m_a, p_a)

    def body(j, carry):
        u = 2 * j
        scores(u, s_a, m_a)
        probs(s_b, m_b, p_b)
        values(u - 2, p_a)
        scores(u + 1, s_b, m_b)
        probs(s_a, m_a, p_a)
        values(u - 1, p_b)
        return carry

    lax.fori_loop(1, n_units // 2, body, 0)
    probs(s_b, m_b, p_b)
    values(n_units - 2, p_a)
    values(n_units - 1, p_b)

    for hp in range(n_pairs):
        o_ref[:, hp * LANES:(hp + 1) * LANES] = o_s[hp]


def _mla_branch(qd, kvd, cache, rope_tabs, kv_bufs, pw, layer, batch, seq):
    latent = cache is not None
    chain_kv = kv_bufs is not None
    n_cache = cache[0].shape[2] if latent else 0
    row = lambda w: pl.BlockSpec((seq, w), lambda b: (b, 0))
    in_specs = [row(Q_LORA), row(KV_PAD), _const_spec(rope_tabs[0].shape)]
    args = [qd, kvd, rope_tabs[0]]
    if latent:
        in_specs += [pl.BlockSpec((1, 1, n_cache, KV_LORA), lambda b: (b, layer, 0, 0)),
                     pl.BlockSpec((1, 1, n_cache, LANES), lambda b: (b, layer, 0, 0))]
        args += list(cache)
        in_specs += [_const_spec((seq, LANES))] * 3
        args += list(rope_tabs[1:])
    consts = [pw["g_q_norm"], pw["g_kv_norm"], pw["w_q_pad"], pw["w_kv_up"]]
    in_specs += [_layer_spec(c, layer) for c in consts]
    args += consts
    aliases = {}
    if chain_kv:
        aliases = {len(args): 1, len(args) + 1: 2}
        in_specs += [pl.BlockSpec(memory_space=pl.ANY)] * 2
        args += list(kv_bufs)
    out_specs = [row(D_MODEL)]
    out_shape = [jax.ShapeDtypeStruct((batch * seq, D_MODEL), BF16)]
    if not latent:
        n_slot, first = (1, layer) if chain_kv else (DEPTH, 0)
        out_specs += [pl.BlockSpec((1, n_slot, seq, KV_LORA), lambda b: (b, first, 0, 0)),
                      pl.BlockSpec((1, n_slot, seq, ROPE_DIM), lambda b: (b, first, 0, 0))]
        out_shape += [jax.ShapeDtypeStruct((batch, DEPTH, seq, KV_LORA), F32),
                      jax.ShapeDtypeStruct((batch, DEPTH, seq, ROPE_DIM), F32)]
    n_keys = seq + n_cache
    scratch = [
        pltpu.VMEM((MLA_HEADS, seq, HEAD_SLOT), BF16),
        pltpu.VMEM((MLA_HEADS, n_keys, HEAD_SLOT), BF16),
        pltpu.VMEM((MLA_HEADS if latent else MLA_HEADS // 2, n_keys, LANES), BF16),
    ]
    if latent:
        tile = (2, Q_BLOCK_ROWS, n_keys)
        stat = (2, Q_BLOCK_ROWS, 1)
        scratch += [pltpu.VMEM((MLA_HEADS // 2, seq, LANES), BF16),
                    pltpu.VMEM(tile, F32), pltpu.VMEM(tile, F32),
                    pltpu.VMEM(stat, F32), pltpu.VMEM(stat, F32),
                    pltpu.VMEM(tile, BF16), pltpu.VMEM(tile, BF16)]
    return pl.pallas_call(
        functools.partial(_mla_kernel, seq=seq, n_cache=n_cache, chain_kv=chain_kv, layer=layer),
        grid=(batch,),
        in_specs=in_specs,
        out_specs=out_specs,
        out_shape=out_shape,
        input_output_aliases=aliases,
        scratch_shapes=scratch,
        compiler_params=_params(("parallel",)),
        name="mla_branch",
    )(*args)


def _fft_kernel(f_ref, cs_ref, dl_ref, o_ref, *, seq):
    scale = 1.0 / math.sqrt(seq * FFT_GROUP_W)
    for g in range(FFT_GROUPS):
        gs = slice(g * FFT_GROUP_W, (g + 1) * FFT_GROUP_W)
        t = _dot(f_ref[:, gs], cs_ref[...]).astype(BF16)
        stack = jnp.concatenate([t[:, :FFT_GROUP_W], t[:, FFT_GROUP_W:]], axis=0)
        o_ref[:, gs] = (_dot(dl_ref[...], stack) * scale).astype(o_ref.dtype)


def _fft_branch(f_in, cs, dl, batch, seq):
    row = pl.BlockSpec((seq, D_MODEL), lambda b: (b, 0))
    return pl.pallas_call(
        functools.partial(_fft_kernel, seq=seq),
        grid=(batch,),
        in_specs=[row, _const_spec(cs.shape), _const_spec(dl.shape)],
        out_specs=row,
        out_shape=jax.ShapeDtypeStruct((batch * seq, D_MODEL), BF16),
        compiler_params=_params(("parallel",)),
        name="fft_branch",
    )(f_in, cs, dl)


def _dft_tables(seq):
    def cos_sin(n):
        idx = np.arange(n, dtype=np.int64)
        ang = 2.0 * np.pi * ((idx[:, None] * idx[None, :]) % n) / n
        return np.cos(ang), np.sin(ang)

    cc, sc = cos_sin(FFT_GROUP_W)
    cl, sl = cos_sin(seq)
    cs = np.concatenate([cc, sc], axis=1).astype(np.float32)
    dl = np.concatenate([cl, -sl], axis=1).astype(np.float32)
    return jnp.asarray(cs).astype(BF16), jnp.asarray(dl).astype(BF16)


def _merge_kernel(y_ref, o_ref, f_ref, gate_ref, x_ref, g1_ref, gp_ref,
                  ws_ref, wm_ref, wf_ref, wo_ref, out_ref, *, mod_row):
    row = mod_row(pl.program_id(0))
    mix = (gate_ref[:, 0:D_MODEL] * _dot(y_ref[...], ws_ref[0])
           + gate_ref[:, D_MODEL:2 * D_MODEL] * _dot(o_ref[...], wm_ref[0])
           + gate_ref[:, 2 * D_MODEL:3 * D_MODEL] * _dot(f_ref[...], wf_ref[0]))
    out = _dot(mix.astype(BF16), wo_ref[0])
    out_ref[...] = x_ref[...] + _mod_row(g1_ref, row) * _rms(out, gp_ref[0])


def _merge(y, o, f, gates, x, mods, pw, layer, seq, latent):
    t = x.shape[0]
    row = lambda w: pl.BlockSpec((ROW_TILE, w), lambda i: (i, 0))
    consts = [pw["g_post_mix"], pw["w_ssd_out"], pw["w_mla_out"], pw["w_fft_out"], pw["w_o"]]
    return pl.pallas_call(
        functools.partial(_merge_kernel, mod_row=_row_of_tile(seq // ROW_TILE, latent)),
        grid=(t // ROW_TILE,),
        in_specs=[row(D_MODEL), row(D_MODEL), row(D_MODEL), row(N_BRANCH * D_MODEL), row(D_MODEL),
                  _mod_spec(layer, 2)] + [_layer_spec(c, layer) for c in consts],
        out_specs=row(D_MODEL),
        out_shape=jax.ShapeDtypeStruct((t, D_MODEL), F32),
        compiler_params=_params(("parallel",)),
        name="merge",
    )(y, o, f, gates, x, mods, *consts)


def _ffn_kernel(x_ref, sh_ref, sc_ref, g2_ref, gpre_ref, gpost_ref,
                wu_ref, cw_ref, cb_ref, wd_ref,
                out_ref, h_s, acc_s, u_a, u_b, *, seq, mod_row):
    row = mod_row(pl.program_id(0))
    h_s[...] = (_rms(x_ref[...], gpre_ref[0]) * (1.0 + _mod_row(sc_ref, row))
                + _mod_row(sh_ref, row)).astype(BF16)

    row8 = lax.broadcasted_iota(jnp.int32, (SUBLANES, FFN_BLOCK), 0)

    def neighbours(u):
        prev = pltpu.roll(u, 1, 0)
        nxt = pltpu.roll(u, FFN_ROWS - 1, 0)
        pp, nn = [], []
        for s0 in range(0, FFN_ROWS, seq):
            pp += [jnp.where(row8 == 0, 0.0, prev[s0:s0 + SUBLANES]), prev[s0 + SUBLANES:s0 + seq]]
            nn += [nxt[s0:s0 + seq - SUBLANES],
                   jnp.where(row8 == SUBLANES - 1, 0.0, nxt[s0 + seq - SUBLANES:s0 + seq])]
        return jnp.concatenate(pp, axis=0), jnp.concatenate(nn, axis=0)

    def cols(k, gate_half):
        c0 = (D_FF if gate_half else 0) + k * FFN_BLOCK
        if isinstance(k, int):
            return slice(c0, c0 + FFN_BLOCK)
        return pl.ds(pl.multiple_of(c0, FFN_BLOCK), FFN_BLOCK)

    def rows(k):
        if isinstance(k, int):
            return slice(k * FFN_BLOCK, (k + 1) * FFN_BLOCK)
        return pl.ds(pl.multiple_of(k * FFN_BLOCK, FFN_BLOCK), FFN_BLOCK)

    def up(k, u_ref):
        h = h_s[...]
        u_ref[0] = _dot(h, wu_ref[0, :, cols(k, False)])
        u_ref[1] = _dot(h, wu_ref[0, :, cols(k, True)])

    def conv(u, cs):
        prev, nxt = neighbours(u)
        return (prev * cw_ref[0, 0:1, cs] + u * cw_ref[0, 1:2, cs] + nxt * cw_ref[0, 2:3, cs]
                + cb_ref[0, :, cs])

    def down(k, u_ref, first=False):
        a = conv(u_ref[0], cols(k, False))
        g = conv(u_ref[1], cols(k, True))
        d = _dot((_silu(a) * g).astype(BF16), wd_ref[0, rows(k), :])
        if first:
            acc_s[...] = d
        else:
            acc_s[...] += d

    up(0, u_a)
    up(1, u_b)
    down(0, u_a, first=True)

    def body(i, carry):
        up(2 * i, u_a)
        down(2 * i - 1, u_b)
        up(2 * i + 1, u_b)
        down(2 * i, u_a)
        return carry

    n_pairs = (FFN_NBLK - 1) // 2
    lax.fori_loop(1, n_pairs, body, 0)
    last = 2 * n_pairs
    up(last, u_a)
    down(last - 1, u_b)
    down(last, u_a)

    out_ref[...] = x_ref[...] + _mod_row(g2_ref, row) * _rms(acc_s[...], gpost_ref[0])


def _ffn(x, mods, pw, layer, seq, latent):
    t = x.shape[0]
    row = pl.BlockSpec((FFN_ROWS, D_MODEL), lambda i: (i, 0))
    consts = [pw["g_pre_ffn"], pw["g_post_ffn"], pw["w_up"], pw["conv_ffn_w"], pw["conv_ffn_b"], pw["w_down"]]
    stage = pltpu.VMEM((2, FFN_ROWS, FFN_BLOCK), F32)
    return pl.pallas_call(
        functools.partial(_ffn_kernel, seq=seq, mod_row=_row_of_tile(max(seq // FFN_ROWS, 1), latent)),
        grid=(t // FFN_ROWS,),
        in_specs=[row, _mod_spec(layer, 3), _mod_spec(layer, 4), _mod_spec(layer, 5)]
        + [_layer_spec(c, layer) for c in consts],
        out_specs=row,
        out_shape=jax.ShapeDtypeStruct((t, D_MODEL), F32),
        scratch_shapes=[pltpu.VMEM((FFN_ROWS, D_MODEL), BF16), pltpu.VMEM((FFN_ROWS, D_MODEL), F32),
                        stage, stage],
        compiler_params=_params(("parallel",)),
        name="conv_mlp",
    )(x, mods, mods, mods, *consts)


def _rope_partner():
    idx = np.arange(ROPE_DIM)
    quarter = ROPE_DIM // 4
    return np.where(idx % (2 * quarter) < quarter, idx + quarter, idx - quarter)


def _pad_q_up(w_q_up):
    w = w_q_up.reshape(DEPTH, Q_LORA, MLA_HEADS, QK_NOPE + ROPE_DIM)
    pe = w[..., QK_NOPE:]
    w = jnp.concatenate([w, pe[..., _rope_partner()]], axis=-1)
    return w.reshape(DEPTH, Q_LORA, MLA_HEADS * HEAD_SLOT).astype(BF16)


def _rope_tables(seq):
    rows = seq // GRID_W
    row = jnp.repeat(jnp.arange(rows), GRID_W)
    col = jnp.tile(jnp.arange(GRID_W), rows)
    quarter = ROPE_DIM // 4
    inv = ROPE_THETA ** (-jnp.arange(quarter, dtype=F32) / quarter)
    ang_row = row.astype(F32)[:, None] * inv
    ang_col = col.astype(F32)[:, None] * inv
    cos = jnp.concatenate([jnp.cos(ang_row)] * 2 + [jnp.cos(ang_col)] * 2, axis=1)
    sin = jnp.concatenate([jnp.sin(ang_row)] * 2 + [jnp.sin(ang_col)] * 2, axis=1)
    lo = (jnp.arange(ROPE_DIM) % (2 * quarter)) < quarter
    sin_signed = jnp.where(lo, -sin, sin)
    scale = ATTN_SCALE * LOG2E
    q_tab = jnp.concatenate([jnp.full((seq, QK_NOPE), scale, F32), scale * cos, scale * sin_signed], axis=1)
    pad = lambda t, fill: jnp.concatenate([t, jnp.full((seq, LANES - ROPE_DIM), fill, F32)], axis=1)
    return (q_tab, pad(cos, 1.0), pad(jnp.where(lo, -sin, 0.0), 0.0), pad(jnp.where(lo, 0.0, sin), 0.0))


def _ctx_q_table():
    scale = ATTN_SCALE * LOG2E
    return jnp.concatenate([jnp.full((1, QK_NOPE + ROPE_DIM), scale, F32),
                            jnp.zeros((1, HEAD_SLOT - QK_NOPE - ROPE_DIM), F32)], axis=1)


def _trunk_pass(x, mods, pw, layer, ctx, outs, tabs, batch, seq):
    latent = ctx is not None
    z, xbc, dt, qd, kvd, f_in, gates = _pre_mix(x, mods, pw, layer, seq, latent)
    y_ssd, ssm = _ssd_branch(z, xbc, dt, ctx[2] if latent else None, outs[2] if outs else None,
                             pw, layer, batch, seq, want_state=not latent)
    mla = _mla_branch(qd, kvd, (ctx[0], ctx[1]) if latent else None, tabs["rope"],
                      (outs[0], outs[1]) if outs else None, pw, layer, batch, seq)
    f = _fft_branch(f_in, tabs["cs"], tabs["dl"], batch, seq)
    x1 = _merge(y_ssd, mla[0], f, gates, x, mods, pw, layer, seq, latent)
    x2 = _ffn(x1, mods, pw, layer, seq, latent)
    if latent:
        return x2, None
    return x2, (mla[1], mla[2], ssm)


def kernel(x_prompt, x_sample, cache_ckv, cache_kpe, state_ssm, c, c_ctx, w_ada, b_ada, g_pre_mix, g_post_mix, g_pre_ffn, g_post_ffn, w_in, w_gate, b_gate, w_o, conv_ssd_w, conv_ssd_b, dt_bias, a_log, d_skip, g_ssd_norm, w_ssd_out, g_q_norm, w_q_up, g_kv_norm, w_kv_up, w_mla_out, w_fft_out, w_up, conv_ffn_w, conv_ffn_b, w_down):
    batch, seq, _ = x_prompt.shape
    dec_batch, dec_seq, _ = x_sample.shape

    cvec = jnp.concatenate([c_ctx[None, :], c, jnp.zeros((SUBLANES - 1 - dec_batch, D_MODEL), F32)], axis=0)
    mods = _ada_params(cvec, w_ada, b_ada)

    vec = lambda v: v[:, None, :]
    pad_lane = lambda v: jnp.pad(v, ((0, 0), (0, 0), (0, LANES - v.shape[-1])))
    pw = {
        "g_pre_mix": vec(g_pre_mix), "g_post_mix": vec(g_post_mix),
        "g_pre_ffn": vec(g_pre_ffn), "g_post_ffn": vec(g_post_ffn),
        "w_in": w_in.astype(BF16),
        "w_gate": w_gate.astype(BF16), "b_gate": vec(b_gate),
        "w_o": w_o.astype(BF16), "w_ssd_out": w_ssd_out.astype(BF16),
        "w_mla_out": w_mla_out.astype(BF16), "w_fft_out": w_fft_out.astype(BF16),
        "conv_ssd_w": conv_ssd_w, "conv_ssd_b": vec(conv_ssd_b),
        "dt_bias": pad_lane(dt_bias.reshape(DEPTH, 1, 2 * SSD_HEADS)),
        "a_log": pad_lane(a_log.reshape(DEPTH, 1, 2 * SSD_HEADS)),
        "d_skip": vec(jnp.repeat(d_skip, SSD_HEAD_DIM, axis=1)),
        "g_ssd_norm": vec(g_ssd_norm),
        "g_q_norm": vec(g_q_norm), "g_kv_norm": vec(g_kv_norm),
        "w_q_pad": _pad_q_up(w_q_up), "w_kv_up": w_kv_up.astype(BF16),
        "w_up": w_up.astype(BF16), "conv_ffn_w": conv_ffn_w, "conv_ffn_b": vec(conv_ffn_b),
        "w_down": w_down.astype(BF16),
    }

    cs, dl_ctx = _dft_tables(seq)
    _, dl_lat = _dft_tables(dec_seq)
    tabs_ctx = {"rope": (_ctx_q_table(),), "cs": cs, "dl": dl_ctx}
    tabs_lat = {"rope": _rope_tables(dec_seq), "cs": cs, "dl": dl_lat}
    cache_kpe_p = jnp.pad(cache_kpe, ((0, 0), (0, 0), (0, 0), (0, LANES - ROPE_DIM)))
    h0_all = state_ssm.reshape(dec_batch, DEPTH, 2, D_INNER, SSD_STATE)
    ctx = (cache_ckv, cache_kpe_p, h0_all)

    y_p = x_prompt.reshape(batch * seq, D_MODEL)
    y_s = x_sample.reshape(dec_batch * dec_seq, D_MODEL)
    outs = None
    for i in range(DEPTH):
        y_p, outs = _trunk_pass(y_p, mods, pw, i, None, outs, tabs_ctx, batch, seq)
        y_s, _ = _trunk_pass(y_s, mods, pw, i, ctx, None, tabs_lat, dec_batch, dec_seq)
    new_ckv, new_kpe, new_ssm = outs
    return (y_p.reshape(batch, seq, D_MODEL), y_s.reshape(dec_batch, dec_seq, D_MODEL),
            new_ckv, new_kpe, new_ssm.reshape(batch, DEPTH, 2, SSD_HEADS, SSD_HEAD_DIM, SSD_STATE))
```

```python
import functools
import math

import numpy as np
import jax
import jax.numpy as jnp
from jax import lax
from jax.experimental import pallas as pl
from jax.experimental.pallas import tpu as pltpu

F32 = jnp.float32
BF16 = jnp.bfloat16

D_MODEL = 1024
DEPTH = 4
GRID_W = 64
EPS = 1e-6
SSD_HEADS = 16
SSD_HEAD_DIM = 64
D_INNER = SSD_HEADS * SSD_HEAD_DIM
SSD_GROUPS = 2
SSD_STATE = 128
SSD_CHUNK = 128
CONV_CH = D_INNER + 2 * SSD_GROUPS * SSD_STATE
MLA_HEADS = 16
Q_LORA = 384
KV_LORA = 256
QK_NOPE = 64
ROPE_DIM = 32
V_DIM = 64
ROPE_THETA = 10000.0
ATTN_SCALE = (QK_NOPE + ROPE_DIM) ** -0.5
FFT_GROUPS = 4
FFT_GROUP_W = 256
D_FF = 2816
N_BRANCH = 3
N_MOD = 6
OFF_XBC = D_INNER
OFF_DT = OFF_XBC + CONV_CH
OFF_QD = OFF_DT + 2 * SSD_HEADS
OFF_KVD = OFF_QD + Q_LORA
OFF_FFT = OFF_KVD + KV_LORA + ROPE_DIM

LANES = 128
SUBLANES = 8
HEAD_SLOT = LANES
KV_PAD = KV_LORA + LANES
DT_PAD = LANES

ROW_TILE = 512
FFT_ROWS = 1024
MERGE_ROWS = 512
FFN_ROWS = 1024
FFN_BLOCK = 256
FFN_NBLK = D_FF // FFN_BLOCK
Q_BLOCK_ROWS = 256
VMEM_LIMIT = 56 * 1024 * 1024
NEG_BIG = -1e30
LOG2E = math.log2(math.e)


def _params(sem, vmem=VMEM_LIMIT):
    return pltpu.CompilerParams(dimension_semantics=sem, vmem_limit_bytes=vmem)


def _sigmoid(x):
    return 1.0 / (1.0 + jnp.exp(-x))


def _silu(x):
    return x * _sigmoid(x)


def _rms(x, g):
    return x * lax.rsqrt(jnp.mean(x * x, axis=-1, keepdims=True) + EPS) * g


def _dot(a, b):
    return jnp.dot(a, b, preferred_element_type=F32)


def _dot_nt(a, b):
    return lax.dot_general(a, b, (((1,), (1,)), ((), ())), preferred_element_type=F32)


def _const_spec(shape):
    nd = len(shape)
    return pl.BlockSpec(shape, lambda *_: (0,) * nd, pipeline_mode=pl.Buffered(1))


def _layer_spec(arr, layer):
    shape = (1,) + tuple(arr.shape[1:])
    nd = len(shape)
    return pl.BlockSpec(shape, lambda *_: (layer,) + (0,) * (nd - 1), pipeline_mode=pl.Buffered(1))


def _mod_spec(layer, k):
    return pl.BlockSpec((1, 1, SUBLANES, D_MODEL), lambda *_: (layer, k, 0, 0), pipeline_mode=pl.Buffered(1))


def _mod_row(ref, row):
    return ref[0, 0, pl.ds(row, 1), :]


def _ada_kernel(c_ref, w_ref, b_ref, o_ref):
    c = c_ref[...]
    s = _silu(c).astype(BF16)
    o_ref[0, 0] = _dot(s, w_ref[0].astype(BF16)) + b_ref[0]


def _ada_params(cvec, w_ada, b_ada):
    return pl.pallas_call(
        _ada_kernel,
        grid=(DEPTH, N_MOD),
        in_specs=[
            pl.BlockSpec((SUBLANES, D_MODEL), lambda l, j: (0, 0)),
            pl.BlockSpec((1, D_MODEL, D_MODEL), lambda l, j: (l, 0, j)),
            pl.BlockSpec((1, 1, D_MODEL), lambda l, j: (l, 0, j)),
        ],
        out_specs=pl.BlockSpec((1, 1, SUBLANES, D_MODEL), lambda l, j: (l, j, 0, 0)),
        out_shape=jax.ShapeDtypeStruct((DEPTH, N_MOD, SUBLANES, D_MODEL), F32),
        compiler_params=_params(("parallel", "parallel")),
        name="ada_params",
    )(cvec, w_ada, b_ada.reshape(DEPTH, 1, N_MOD * D_MODEL))


def _pre_kernel(x_ref, sh_ref, sc_ref, g_ref, wi_ref, wg_ref, bg_ref,
                z_ref, xbc_ref, dt_ref, qd_ref, kvd_ref, fin_ref, gate_ref, *, mod_row):
    row = mod_row(pl.program_id(0))
    x = x_ref[...]
    h = (_rms(x, g_ref[0]) * (1.0 + _mod_row(sc_ref, row)) + _mod_row(sh_ref, row)).astype(BF16)

    def seg(out_ref, w_ref, start, width, bias_ref=None):
        step = 512
        for c0 in range(0, width, step):
            w = min(step, width - c0)
            r = _dot(h, w_ref[0, :, start + c0:start + c0 + w])
            if bias_ref is not None:
                r = _sigmoid(r + bias_ref[0, :, c0:c0 + w])
            out_ref[:, c0:c0 + w] = r.astype(out_ref.dtype)

    seg(z_ref, wi_ref, 0, D_INNER)
    seg(xbc_ref, wi_ref, OFF_XBC, CONV_CH)
    t = _dot(h, wi_ref[0, :, OFF_DT:])
    dt_ref[...] = t[:, 0:DT_PAD]
    qd_ref[...] = t[:, OFF_QD - OFF_DT:OFF_KVD - OFF_DT]
    kvd_ref[...] = t[:, OFF_KVD - OFF_DT:OFF_KVD - OFF_DT + KV_PAD]
    fin_ref[...] = t[:, OFF_FFT - OFF_DT:].astype(fin_ref.dtype)
    seg(gate_ref, wg_ref, 0, N_BRANCH * D_MODEL, bg_ref)


def _row_of_tile(rows_per_batch, latent):
    if latent:
        return lambda i: 1 + i // rows_per_batch
    return lambda i: 0


def _pre_mix(x, mods, pw, layer, seq, latent):
    t = x.shape[0]

    def row_spec(width):
        return pl.BlockSpec((ROW_TILE, width), lambda i: (i, 0))

    widths = (D_INNER, CONV_CH, DT_PAD, Q_LORA, KV_PAD, D_MODEL, N_BRANCH * D_MODEL)
    dtypes = (F32, F32, F32, F32, F32, BF16, F32)
    consts = [pw["g_pre_mix"], pw["w_in"], pw["w_gate"], pw["b_gate"]]
    return pl.pallas_call(
        functools.partial(_pre_kernel, mod_row=_row_of_tile(seq // ROW_TILE, latent)),
        grid=(t // ROW_TILE,),
        in_specs=[row_spec(D_MODEL), _mod_spec(layer, 0), _mod_spec(layer, 1)]
        + [_layer_spec(c, layer) for c in consts],
        out_specs=[row_spec(w) for w in widths],
        out_shape=[jax.ShapeDtypeStruct((t, w), d) for w, d in zip(widths, dtypes)],
        compiler_params=_params(("parallel",)),
        name="pre_mix",
    )(x, mods, mods, *consts)


def _split3(x):
    hi = x.astype(BF16)
    r1 = x - hi.astype(F32)
    mid = r1.astype(BF16)
    lo = (r1 - mid.astype(F32)).astype(BF16)
    return hi, mid, lo


def _ssd_kernel(*refs, seq, has_h0, want_state, chain_state, layer):
    it = iter(refs)
    z_ref, xbc_ref, dt_ref = next(it), next(it), next(it)
    h0_ref = next(it) if has_h0 else None
    cw_ref, cb_ref, dtb_ref, alog_ref, dsk_ref, gn_ref = (next(it) for _ in range(6))
    if chain_state:
        next(it)
    y_ref = next(it)
    hf_ref = next(it) if want_state else None
    bc_s, xs_s, yf_s, yb_s, ld_s, da_s, h_s = (next(it) for _ in range(7))
    lhs_a, lhs_b, bwt_a, bwt_b, cd_a, cd_b = (next(it) for _ in range(6))

    nc = seq // SSD_CHUNK
    rb = SSD_CHUNK

    cstep = 512
    for r0 in range(0, seq, rb):
        for c0 in range(0, CONV_CH, cstep):
            cs = slice(c0, c0 + cstep)
            cur = xbc_ref[r0:r0 + rb, cs]
            rid = lax.broadcasted_iota(jnp.int32, (rb, cstep), 0)
            prev = pltpu.roll(cur, 1, 0)
            if r0 > 0:
                prev = jnp.where(rid == 0, xbc_ref[r0 - 1:r0, cs], prev)
            else:
                prev = jnp.where(rid == 0, 0.0, prev)
            nxt = pltpu.roll(cur, rb - 1, 0)
            if r0 + rb < seq:
                nxt = jnp.where(rid == rb - 1, xbc_ref[r0 + rb:r0 + rb + 1, cs], nxt)
            else:
                nxt = jnp.where(rid == rb - 1, 0.0, nxt)
            u = (prev * cw_ref[0, 0:1, cs] + cur * cw_ref[0, 1:2, cs] + nxt * cw_ref[0, 2:3, cs]
                 + cb_ref[0, :, cs])
            u = _silu(u)
            if c0 < D_INNER:
                xs_s[r0:r0 + rb, cs] = u.astype(BF16)
                yf_s[r0:r0 + rb, cs] = u * dsk_ref[0, :, cs]
            else:
                bc_s[r0:r0 + rb, c0 - D_INNER:c0 - D_INNER + cstep] = u

    a_row = -jnp.exp(alog_ref[0]) * LOG2E
    for r0 in range(0, seq, rb):
        v = dt_ref[r0:r0 + rb, :] + dtb_ref[0]
        dtv = jnp.maximum(v, 0.0) + jnp.log1p(jnp.exp(-jnp.abs(v)))
        ld_s[r0:r0 + rb, :] = jnp.log2(dtv)
        da_s[r0:r0 + rb, :] = dtv * a_row

    for d in range(2):
        if has_h0:
            for hp in range(SSD_HEADS // 2):
                blk = h0_ref[0, 0, d, hp * LANES:(hp + 1) * LANES, :]
                h_s[d, :, hp * LANES:(hp + 1) * LANES] = blk.T
        else:
            h_s[d] = jnp.zeros((SSD_STATE, D_INNER), F32)

    ri = lax.broadcasted_iota(jnp.int32, (rb, rb), 0)
    ci = lax.broadcasted_iota(jnp.int32, (rb, rb), 1)
    lower = ri >= ci
    upper = ri <= ci
    tri_f = jnp.where(lower, 1.0, 0.0).astype(BF16)
    tri_b = jnp.where(upper, 1.0, 0.0).astype(BF16)
    first_half = ci < SSD_HEAD_DIM

    def rows_of(step, d):
        c = step if d == 0 else nc - 1 - step
        return c * rb if isinstance(c, int) else pl.multiple_of(c * rb, rb)

    def build(step, stage):
        lhs_ref, bwt_ref, cd_ref = stage
        for d in range(2):
            r = rows_of(step, d)
            mask = lower if d == 0 else upper
            tri = tri_f if d == 0 else tri_b
            end = rb - 1 if d == 0 else 0
            hi, mid, lo = _split3(da_s[pl.ds(r, rb), :])
            acum = _dot(tri, hi) + _dot(tri, mid) + _dot(tri, lo)
            acum_t = acum.T
            rl_t = acum_t - ld_s[pl.ds(r, rb), :].T
            w_t = jnp.exp2(acum_t[:, end:end + 1] - rl_t)
            for g in range(SSD_GROUPS):
                b_g = bc_s[pl.ds(r, rb), g * SSD_STATE:(g + 1) * SSD_STATE]
                c_g = bc_s[pl.ds(r, rb), (SSD_GROUPS + g) * SSD_STATE:(SSD_GROUPS + g + 1) * SSD_STATE]
                cb = _dot_nt(c_g.astype(BF16), b_g.astype(BF16))
                b_t = b_g.T
                for e in range(g * 8, g * 8 + 8):
                    k = d * SSD_HEADS + e
                    colb = jnp.broadcast_to(acum[:, k:k + 1], (rb, rb))
                    m_e = jnp.exp2(jnp.where(mask, colb - rl_t[k:k + 1, :], NEG_BIG)) * cb
                    e_col = jnp.exp2(colb)
                    lhs_ref[d, e] = jnp.concatenate([m_e.astype(BF16), (c_g * e_col).astype(BF16)], axis=1)
                    bwt_ref[d, e] = (b_t * w_t[k:k + 1, :]).astype(BF16)
                    cd_ref[d, e] = e_col[end:end + 1, :]

    def apply(step, stage):
        lhs_ref, bwt_ref, cd_ref = stage
        for d in range(2):
            r = rows_of(step, d)
            for hp in range(SSD_HEADS // 2):
                ls = slice(hp * LANES, (hp + 1) * LANES)
                x_pair = xs_s[pl.ds(r, rb), ls]
                h_pair = h_s[d, :, ls]
                rhs = jnp.concatenate([x_pair, h_pair.astype(BF16)], axis=0)
                y0 = _dot(lhs_ref[d, 2 * hp], rhs)
                y1 = _dot(lhs_ref[d, 2 * hp + 1], rhs)
                s0 = _dot(bwt_ref[d, 2 * hp], x_pair)
                s1 = _dot(bwt_ref[d, 2 * hp + 1], x_pair)
                y_pair = jnp.where(first_half, y0, y1)
                if d == 0:
                    yf_s[pl.ds(r, rb), ls] = yf_s[pl.ds(r, rb), ls] + y_pair
                else:
                    yb_s[pl.ds(r, rb), ls] = y_pair
                cd = jnp.where(first_half[0:1, :], cd_ref[d, 2 * hp], cd_ref[d, 2 * hp + 1])
                h_s[d, :, ls] = h_pair * cd + jnp.where(first_half, s0, s1)

    stage_a = (lhs_a, bwt_a, cd_a)
    stage_b = (lhs_b, bwt_b, cd_b)
    build(0, stage_a)

    def body(j, carry):
        t = 2 * j
        build(t + 1, stage_b)
        apply(t, stage_a)
        build(t + 2, stage_a)
        apply(t + 1, stage_b)
        return carry

    n_loop = (nc - 2) // 2
    if n_loop > 0:
        lax.fori_loop(0, n_loop, body, 0)
    build(nc - 1, stage_b)
    apply(nc - 2, stage_a)
    apply(nc - 1, stage_b)

    for r0 in range(0, seq, rb):
        y = (yf_s[r0:r0 + rb, :] + yb_s[r0:r0 + rb, :]) * _silu(z_ref[r0:r0 + rb, :])
        y_ref[r0:r0 + rb, :] = _rms(y, gn_ref[0]).astype(y_ref.dtype)

    if want_state:
        for d in range(2):
            for hp in range(SSD_HEADS // 2):
                slot = 0 if chain_state else layer
                hf_ref[0, slot, d, hp * LANES:(hp + 1) * LANES, :] = h_s[d, :, hp * LANES:(hp + 1) * LANES].T
        if not chain_state:
            for other in range(DEPTH):
                if other != layer:
                    hf_ref[0, other] = jnp.zeros((2, D_INNER, SSD_STATE), F32)


def _ssd_branch(z, xbc, dt, h0, state_buf, pw, layer, batch, seq, want_state):
    has_h0 = h0 is not None
    chain_state = state_buf is not None
    row = lambda w: pl.BlockSpec((seq, w), lambda b: (b, 0))
    state_spec = pl.BlockSpec((1, 1, 2, D_INNER, SSD_STATE), lambda b: (b, layer, 0, 0, 0))
    in_specs = [row(D_INNER), row(CONV_CH), row(DT_PAD)]
    args = [z, xbc, dt]
    if has_h0:
        in_specs.append(state_spec)
        args.append(h0)
    consts = [pw["conv_ssd_w"], pw["conv_ssd_b"], pw["dt_bias"], pw["a_log"], pw["d_skip"], pw["g_ssd_norm"]]
    in_specs += [_layer_spec(c, layer) for c in consts]
    args += consts
    aliases = {}
    if chain_state:
        aliases = {len(args): 1}
        in_specs.append(pl.BlockSpec(memory_space=pl.ANY))
        args.append(state_buf)
    out_specs = [row(D_INNER)]
    out_shape = [jax.ShapeDtypeStruct((batch * seq, D_INNER), BF16)]
    if want_state:
        out_specs.append(state_spec if chain_state else
                         pl.BlockSpec((1, DEPTH, 2, D_INNER, SSD_STATE), lambda b: (b, 0, 0, 0, 0)))
        out_shape.append(jax.ShapeDtypeStruct((batch, DEPTH, 2, D_INNER, SSD_STATE), F32))
    res = pl.pallas_call(
        functools.partial(_ssd_kernel, seq=seq, has_h0=has_h0, want_state=want_state,
                          chain_state=chain_state, layer=layer),
        grid=(batch,),
        in_specs=in_specs,
        out_specs=out_specs,
        out_shape=out_shape,
        input_output_aliases=aliases,
        scratch_shapes=[
            pltpu.VMEM((seq, CONV_CH - D_INNER), F32),
            pltpu.VMEM((seq, D_INNER), BF16),
            pltpu.VMEM((seq, D_INNER), F32),
            pltpu.VMEM((seq, D_INNER), F32),
            pltpu.VMEM((seq, DT_PAD), F32),
            pltpu.VMEM((seq, DT_PAD), F32),
            pltpu.VMEM((2, SSD_STATE, D_INNER), F32),
        ] + 2 * [pltpu.VMEM((2, SSD_HEADS, SSD_CHUNK, 2 * SSD_CHUNK), BF16)]
          + 2 * [pltpu.VMEM((2, SSD_HEADS, SSD_STATE, SSD_CHUNK), BF16)]
          + 2 * [pltpu.VMEM((2, SSD_HEADS, 1, SSD_STATE), F32)],
        compiler_params=_params(("parallel",)),
        name="ssd_branch",
    )(*args)
    return (res[0], res[1]) if want_state else (res[0], None)


def _mla_kernel(*refs, seq, n_cache, chain_kv, layer):
    it = iter(refs)
    qd_ref, kvd_ref = next(it), next(it)
    latent = n_cache > 0
    qtab_ref = next(it)
    if latent:
        cckv_ref, ckpe_ref = next(it), next(it)
        kcos_ref, ksn_ref, ksp_ref = (next(it) for _ in range(3))
    gq_ref, gkv_ref, wq_ref, wkv_ref = (next(it) for _ in range(4))
    if chain_kv:
        next(it), next(it)
    o_ref = next(it)
    if not latent:
        ckv_out_ref, kpe_out_ref = next(it), next(it)
    q_s, k_s, v_s = (next(it) for _ in range(3))
    if latent:
        o_s, s_a, s_b, m_a, m_b, p_a, p_b = (next(it) for _ in range(7))

    rb = Q_BLOCK_ROWS
    n_pairs = MLA_HEADS // 2
    lane = lax.broadcasted_iota(jnp.int32, (rb, LANES), 1)
    first_half = lane < V_DIM
    ones_even = jnp.where(lane == V_DIM, 1.0, 0.0)
    ones_odd = jnp.where(lane == 0, 1.0, 0.0)

    def pair_cols(i):
        return slice(2 * i * HEAD_SLOT, 2 * (i + 1) * HEAD_SLOT)

    for r0 in range(0, seq, rb):
        qn = _rms(qd_ref[r0:r0 + rb, :], gq_ref[0]).astype(BF16)
        tab = qtab_ref[r0:r0 + rb, :] if latent else qtab_ref[...]
        for hp in range(n_pairs):
            qq = _dot(qn, wq_ref[0, :, pair_cols(hp)])
            q_s[2 * hp, r0:r0 + rb, :] = (qq[:, :HEAD_SLOT] * tab).astype(BF16)
            q_s[2 * hp + 1, r0:r0 + rb, :] = (qq[:, HEAD_SLOT:] * tab).astype(BF16)

    def put_keys(row0, ckv_n, kpe_tile):
        kpe_slot = pltpu.roll(kpe_tile, QK_NOPE, 1) + pltpu.roll(kpe_tile, QK_NOPE + ROPE_DIM, 1)
        cb16 = ckv_n.astype(BF16)
        for hp in range(n_pairs):
            kv = _dot(cb16, wkv_ref[0, :, pair_cols(hp)])
            kv_e, kv_o = kv[:, :HEAD_SLOT], kv[:, HEAD_SLOT:]
            k_s[2 * hp, row0:row0 + rb, :] = (jnp.where(first_half, kv_e, 0.0) + kpe_slot).astype(BF16)
            k_s[2 * hp + 1, row0:row0 + rb, :] = (jnp.where(first_half, kv_o, 0.0) + kpe_slot).astype(BF16)
            v_e = pltpu.roll(kv_e, V_DIM, 1)
            if latent:
                v_s[2 * hp, row0:row0 + rb, :] = (jnp.where(first_half, v_e, 0.0) + ones_even).astype(BF16)
                v_s[2 * hp + 1, row0:row0 + rb, :] = (jnp.where(first_half, 0.0, kv_o) + ones_odd).astype(BF16)
            else:
                v_s[hp, row0:row0 + rb, :] = jnp.where(first_half, v_e, kv_o).astype(BF16)

    for r0 in range(0, seq, rb):
        ckv_n = _rms(kvd_ref[r0:r0 + rb, 0:KV_LORA], gkv_ref[0])
        kpe = jnp.where(lane < ROPE_DIM, kvd_ref[r0:r0 + rb, KV_LORA:KV_PAD], 0.0)
        if latent:
            kpe = (kpe * kcos_ref[r0:r0 + rb, :]
                   + pltpu.roll(kpe, LANES - 8, 1) * ksn_ref[r0:r0 + rb, :]
                   + pltpu.roll(kpe, 8, 1) * ksp_ref[r0:r0 + rb, :])
        else:
            slot = 0 if chain_kv else layer
            ckv_out_ref[0, slot, r0:r0 + rb, :] = ckv_n
            kpe_out_ref[0, slot, r0:r0 + rb, :] = kpe[:, 0:ROPE_DIM]
            if not chain_kv:
                for other in range(DEPTH):
                    if other != layer:
                        ckv_out_ref[0, other, r0:r0 + rb, :] = jnp.zeros((rb, KV_LORA), F32)
                        kpe_out_ref[0, other, r0:r0 + rb, :] = jnp.zeros((rb, ROPE_DIM), F32)
        put_keys(r0, ckv_n, kpe)
    for r0 in range(0, n_cache, rb):
        put_keys(seq + r0, cckv_ref[0, 0, r0:r0 + rb, :], ckpe_ref[0, 0, r0:r0 + rb, :])

    def normalise(o0, o1):
        inv0 = 1.0 / o0[:, V_DIM:V_DIM + 1]
        inv1 = 1.0 / o1[:, 0:1]
        return jnp.where(first_half, o0 * inv0, o1 * inv1).astype(BF16)

    if not latent:
        for hp in range(n_pairs):
            outs = []
            for j in range(2):
                h = 2 * hp + j
                s = _dot_nt(q_s[h], k_s[h])
                p = jnp.exp2(s - jnp.max(s, axis=-1, keepdims=True))
                l = jnp.sum(p, axis=-1, keepdims=True)
                outs.append(_dot(p.astype(BF16), v_s[hp]) * (1.0 / l))
            o_ref[:, hp * LANES:(hp + 1) * LANES] = jnp.where(first_half, outs[0], outs[1]).astype(BF16)
        return

    nqb = seq // rb
    n_units = n_pairs * nqb

    def unit(u):
        if isinstance(u, int):
            return u // nqb, (u % nqb) * rb
        return u // nqb, pl.multiple_of((u % nqb) * rb, rb)

    def scores(u, s_ref, m_ref):
        hp, r = unit(u)
        for j in range(2):
            s = _dot_nt(q_s[2 * hp + j, pl.ds(r, rb), :], k_s[2 * hp + j])
            s_ref[j] = s
            m_ref[j] = jnp.max(s, axis=-1, keepdims=True)

    def probs(s_ref, m_ref, p_ref):
        for j in range(2):
            p_ref[j] = jnp.exp2(s_ref[j] - m_ref[j]).astype(BF16)

    def values(u, p_ref):
        hp, r = unit(u)
        o_s[hp, pl.ds(r, rb), :] = normalise(_dot(p_ref[0], v_s[2 * hp]), _dot(p_ref[1], v_s[2 * hp + 1]))

    scores(0, s_a, m_a)
    scores(1, s_b, m_b)
    probs(s_a, m_a, p_a)

    def body(j, carry):
        u = 2 * j
        scores(u, s_a, m_a)
        probs(s_b, m_b, p_b)
        values(u - 2, p_a)
        scores(u + 1, s_b, m_b)
        probs(s_a, m_a, p_a)
        values(u - 1, p_b)
        return carry

    lax.fori_loop(1, n_units // 2, body, 0)
    probs(s_b, m_b, p_b)
    values(n_units - 2, p_a)
    values(n_units - 1, p_b)

    for hp in range(n_pairs):
        o_ref[:, hp * LANES:(hp + 1) * LANES] = o_s[hp]


def _mla_branch(qd, kvd, cache, rope_tabs, kv_bufs, pw, layer, batch, seq):
    latent = cache is not None
    chain_kv = kv_bufs is not None
    n_cache = cache[0].shape[2] if latent else 0
    row = lambda w: pl.BlockSpec((seq, w), lambda b: (b, 0))
    in_specs = [row(Q_LORA), row(KV_PAD), _const_spec(rope_tabs[0].shape)]
    args = [qd, kvd, rope_tabs[0]]
    if latent:
        in_specs += [pl.BlockSpec((1, 1, n_cache, KV_LORA), lambda b: (b, layer, 0, 0)),
                     pl.BlockSpec((1, 1, n_cache, LANES), lambda b: (b, layer, 0, 0))]
        args += list(cache)
        in_specs += [_const_spec((seq, LANES))] * 3
        args += list(rope_tabs[1:])
    consts = [pw["g_q_norm"], pw["g_kv_norm"], pw["w_q_pad"], pw["w_kv_up"]]
    in_specs += [_layer_spec(c, layer) for c in consts]
    args += consts
    aliases = {}
    if chain_kv:
        aliases = {len(args): 1, len(args) + 1: 2}
        in_specs += [pl.BlockSpec(memory_space=pl.ANY)] * 2
        args += list(kv_bufs)
    out_specs = [row(D_MODEL)]
    out_shape = [jax.ShapeDtypeStruct((batch * seq, D_MODEL), BF16)]
    if not latent:
        n_slot, first = (1, layer) if chain_kv else (DEPTH, 0)
        out_specs += [pl.BlockSpec((1, n_slot, seq, KV_LORA), lambda b: (b, first, 0, 0)),
                      pl.BlockSpec((1, n_slot, seq, ROPE_DIM), lambda b: (b, first, 0, 0))]
        out_shape += [jax.ShapeDtypeStruct((batch, DEPTH, seq, KV_LORA), F32),
                      jax.ShapeDtypeStruct((batch, DEPTH, seq, ROPE_DIM), F32)]
    n_keys = seq + n_cache
    scratch = [
        pltpu.VMEM((MLA_HEADS, seq, HEAD_SLOT), BF16),
        pltpu.VMEM((MLA_HEADS, n_keys, HEAD_SLOT), BF16),
        pltpu.VMEM((MLA_HEADS if latent else MLA_HEADS // 2, n_keys, LANES), BF16),
    ]
    if latent:
        tile = (2, Q_BLOCK_ROWS, n_keys)
        stat = (2, Q_BLOCK_ROWS, 1)
        scratch += [pltpu.VMEM((MLA_HEADS // 2, seq, LANES), BF16),
                    pltpu.VMEM(tile, F32), pltpu.VMEM(tile, F32),
                    pltpu.VMEM(stat, F32), pltpu.VMEM(stat, F32),
                    pltpu.VMEM(tile, BF16), pltpu.VMEM(tile, BF16)]
    return pl.pallas_call(
        functools.partial(_mla_kernel, seq=seq, n_cache=n_cache, chain_kv=chain_kv, layer=layer),
        grid=(batch,),
        in_specs=in_specs,
        out_specs=out_specs,
        out_shape=out_shape,
        input_output_aliases=aliases,
        scratch_shapes=scratch,
        compiler_params=_params(("parallel",)),
        name="mla_branch",
    )(*args)


def _fft_kernel(f_ref, cs_ref, dl_ref, o_ref, *, seq, n_seq):
    scale = 1.0 / math.sqrt(seq * FFT_GROUP_W)
    for s in range(n_seq):
        rs = slice(s * seq, (s + 1) * seq)
        for g in range(FFT_GROUPS):
            gs = slice(g * FFT_GROUP_W, (g + 1) * FFT_GROUP_W)
            t = _dot(f_ref[rs, gs], cs_ref[...]).astype(BF16)
            stack = jnp.concatenate([t[:, :FFT_GROUP_W], t[:, FFT_GROUP_W:]], axis=0)
            o_ref[rs, gs] = (_dot(dl_ref[...], stack) * scale).astype(o_ref.dtype)


def _fft_branch(f_in, cs, dl, batch, seq):
    n_seq = max(FFT_ROWS // seq, 1)
    row = pl.BlockSpec((n_seq * seq, D_MODEL), lambda b: (b, 0))
    return pl.pallas_call(
        functools.partial(_fft_kernel, seq=seq, n_seq=n_seq),
        grid=(batch // n_seq,),
        in_specs=[row, _const_spec(cs.shape), _const_spec(dl.shape)],
        out_specs=row,
        out_shape=jax.ShapeDtypeStruct((batch * seq, D_MODEL), BF16),
        compiler_params=_params(("parallel",)),
        name="fft_branch",
    )(f_in, cs, dl)


def _dft_tables(seq):
    def cos_sin(n):
        idx = np.arange(n, dtype=np.int64)
        ang = 2.0 * np.pi * ((idx[:, None] * idx[None, :]) % n) / n
        return np.cos(ang), np.sin(ang)

    cc, sc = cos_sin(FFT_GROUP_W)
    cl, sl = cos_sin(seq)
    cs = np.concatenate([cc, sc], axis=1).astype(np.float32)
    dl = np.concatenate([cl, -sl], axis=1).astype(np.float32)
    return jnp.asarray(cs).astype(BF16), jnp.asarray(dl).astype(BF16)


def _merge_kernel(y_ref, o_ref, f_ref, gate_ref, x_ref, g1_ref, gp_ref,
                  ws_ref, wm_ref, wf_ref, wo_ref, out_ref, *, mod_row):
    row = mod_row(pl.program_id(0))
    mix = (gate_ref[:, 0:D_MODEL] * _dot(y_ref[...], ws_ref[0])
           + gate_ref[:, D_MODEL:2 * D_MODEL] * _dot(o_ref[...], wm_ref[0])
           + gate_ref[:, 2 * D_MODEL:3 * D_MODEL] * _dot(f_ref[...], wf_ref[0]))
    out = _dot(mix.astype(BF16), wo_ref[0])
    out_ref[...] = x_ref[...] + _mod_row(g1_ref, row) * _rms(out, gp_ref[0])


def _merge(y, o, f, gates, x, mods, pw, layer, seq, latent):
    t = x.shape[0]
    row = lambda w: pl.BlockSpec((MERGE_ROWS, w), lambda i: (i, 0))
    consts = [pw["g_post_mix"], pw["w_ssd_out"], pw["w_mla_out"], pw["w_fft_out"], pw["w_o"]]
    return pl.pallas_call(
        functools.partial(_merge_kernel, mod_row=_row_of_tile(max(seq // MERGE_ROWS, 1), latent)),
        grid=(t // MERGE_ROWS,),
        in_specs=[row(D_MODEL), row(D_MODEL), row(D_MODEL), row(N_BRANCH * D_MODEL), row(D_MODEL),
                  _mod_spec(layer, 2)] + [_layer_spec(c, layer) for c in consts],
        out_specs=row(D_MODEL),
        out_shape=jax.ShapeDtypeStruct((t, D_MODEL), F32),
        compiler_params=_params(("parallel",)),
        name="merge",
    )(y, o, f, gates, x, mods, *consts)


def _ffn_kernel(x_ref, sh_ref, sc_ref, g2_ref, gpre_ref, gpost_ref,
                wu_ref, cw_ref, cb_ref, wd_ref,
                out_ref, h_s, acc_s, u_a, u_b, *, seq, mod_row):
    row = mod_row(pl.program_id(0))
    h_s[...] = (_rms(x_ref[...], gpre_ref[0]) * (1.0 + _mod_row(sc_ref, row))
                + _mod_row(sh_ref, row)).astype(BF16)

    row8 = lax.broadcasted_iota(jnp.int32, (SUBLANES, FFN_BLOCK), 0)

    def neighbours(u):
        prev = pltpu.roll(u, 1, 0)
        nxt = pltpu.roll(u, FFN_ROWS - 1, 0)
        pp, nn = [], []
        for s0 in range(0, FFN_ROWS, seq):
            pp += [jnp.where(row8 == 0, 0.0, prev[s0:s0 + SUBLANES]), prev[s0 + SUBLANES:s0 + seq]]
            nn += [nxt[s0:s0 + seq - SUBLANES],
                   jnp.where(row8 == SUBLANES - 1, 0.0, nxt[s0 + seq - SUBLANES:s0 + seq])]
        return jnp.concatenate(pp, axis=0), jnp.concatenate(nn, axis=0)

    def cols(k, gate_half):
        c0 = (D_FF if gate_half else 0) + k * FFN_BLOCK
        if isinstance(k, int):
            return slice(c0, c0 + FFN_BLOCK)
        return pl.ds(pl.multiple_of(c0, FFN_BLOCK), FFN_BLOCK)

    def rows(k):
        if isinstance(k, int):
            return slice(k * FFN_BLOCK, (k + 1) * FFN_BLOCK)
        return pl.ds(pl.multiple_of(k * FFN_BLOCK, FFN_BLOCK), FFN_BLOCK)

    def up(k, u_ref):
        h = h_s[...]
        u_ref[0] = _dot(h, wu_ref[0, :, cols(k, False)])
        u_ref[1] = _dot(h, wu_ref[0, :, cols(k, True)])

    def conv(u, cs):
        prev, nxt = neighbours(u)
        return (prev * cw_ref[0, 0:1, cs] + u * cw_ref[0, 1:2, cs] + nxt * cw_ref[0, 2:3, cs]
                + cb_ref[0, :, cs])

    def down(k, u_ref, first=False):
        a = conv(u_ref[0], cols(k, False))
        g = conv(u_ref[1], cols(k, True))
        d = _dot((_silu(a) * g).astype(BF16), wd_ref[0, rows(k), :])
        if first:
            acc_s[...] = d
        else:
            acc_s[...] += d

    up(0, u_a)
    up(1, u_b)
    down(0, u_a, first=True)

    def body(i, carry):
        up(2 * i, u_a)
        down(2 * i - 1, u_b)
        up(2 * i + 1, u_b)
        down(2 * i, u_a)
        return carry

    n_pairs = (FFN_NBLK - 1) // 2
    lax.fori_loop(1, n_pairs, body, 0)
    last = 2 * n_pairs
    up(last, u_a)
    down(last - 1, u_b)
    down(last, u_a)

    out_ref[...] = x_ref[...] + _mod_row(g2_ref, row) * _rms(acc_s[...], gpost_ref[0])


def _ffn(x, mods, pw, layer, seq, latent):
    t = x.shape[0]
    row = pl.BlockSpec((FFN_ROWS, D_MODEL), lambda i: (i, 0))
    consts = [pw["g_pre_ffn"], pw["g_post_ffn"], pw["w_up"], pw["conv_ffn_w"], pw["conv_ffn_b"], pw["w_down"]]
    stage = pltpu.VMEM((2, FFN_ROWS, FFN_BLOCK), F32)
    return pl.pallas_call(
        functools.partial(_ffn_kernel, seq=seq, mod_row=_row_of_tile(max(seq // FFN_ROWS, 1), latent)),
        grid=(t // FFN_ROWS,),
        in_specs=[row, _mod_spec(layer, 3), _mod_spec(layer, 4), _mod_spec(layer, 5)]
        + [_layer_spec(c, layer) for c in consts],
        out_specs=row,
        out_shape=jax.ShapeDtypeStruct((t, D_MODEL), F32),
        scratch_shapes=[pltpu.VMEM((FFN_ROWS, D_MODEL), BF16), pltpu.VMEM((FFN_ROWS, D_MODEL), F32),
                        stage, stage],
        compiler_params=_params(("parallel",)),
        name="conv_mlp",
    )(x, mods, mods, mods, *consts)


def _rope_partner():
    idx = np.arange(ROPE_DIM)
    quarter = ROPE_DIM // 4
    return np.where(idx % (2 * quarter) < quarter, idx + quarter, idx - quarter)


def _pad_q_up(w_q_up):
    w = w_q_up.reshape(DEPTH, Q_LORA, MLA_HEADS, QK_NOPE + ROPE_DIM)
    pe = w[..., QK_NOPE:]
    w = jnp.concatenate([w, pe[..., _rope_partner()]], axis=-1)
    return w.reshape(DEPTH, Q_LORA, MLA_HEADS * HEAD_SLOT).astype(BF16)


def _rope_tables(seq):
    rows = seq // GRID_W
    row = jnp.repeat(jnp.arange(rows), GRID_W)
    col = jnp.tile(jnp.arange(GRID_W), rows)
    quarter = ROPE_DIM // 4
    inv = ROPE_THETA ** (-jnp.arange(quarter, dtype=F32) / quarter)
    ang_row = row.astype(F32)[:, None] * inv
    ang_col = col.astype(F32)[:, None] * inv
    cos = jnp.concatenate([jnp.cos(ang_row)] * 2 + [jnp.cos(ang_col)] * 2, axis=1)
    sin = jnp.concatenate([jnp.sin(ang_row)] * 2 + [jnp.sin(ang_col)] * 2, axis=1)
    lo = (jnp.arange(ROPE_DIM) % (2 * quarter)) < quarter
    sin_signed = jnp.where(lo, -sin, sin)
    scale = ATTN_SCALE * LOG2E
    q_tab = jnp.concatenate([jnp.full((seq, QK_NOPE), scale, F32), scale * cos, scale * sin_signed], axis=1)
    pad = lambda t, fill: jnp.concatenate([t, jnp.full((seq, LANES - ROPE_DIM), fill, F32)], axis=1)
    return (q_tab, pad(cos, 1.0), pad(jnp.where(lo, -sin, 0.0), 0.0), pad(jnp.where(lo, 0.0, sin), 0.0))


def _ctx_q_table():
    scale = ATTN_SCALE * LOG2E
    return jnp.concatenate([jnp.full((1, QK_NOPE + ROPE_DIM), scale, F32),
                            jnp.zeros((1, HEAD_SLOT - QK_NOPE - ROPE_DIM), F32)], axis=1)


def _trunk_pass(x, mods, pw, layer, ctx, outs, tabs, batch, seq):
    latent = ctx is not None
    z, xbc, dt, qd, kvd, f_in, gates = _pre_mix(x, mods, pw, layer, seq, latent)
    y_ssd, ssm = _ssd_branch(z, xbc, dt, ctx[2] if latent else None, outs[2] if outs else None,
                             pw, layer, batch, seq, want_state=not latent)
    mla = _mla_branch(qd, kvd, (ctx[0], ctx[1]) if latent else None, tabs["rope"],
                      (outs[0], outs[1]) if outs else None, pw, layer, batch, seq)
    f = _fft_branch(f_in, tabs["cs"], tabs["dl"], batch, seq)
    x1 = _merge(y_ssd, mla[0], f, gates, x, mods, pw, layer, seq, latent)
    x2 = _ffn(x1, mods, pw, layer, seq, latent)
    if latent:
        return x2, None
    return x2, (mla[1], mla[2], ssm)


def kernel(x_prompt, x_sample, cache_ckv, cache_kpe, state_ssm, c, c_ctx, w_ada, b_ada, g_pre_mix, g_post_mix, g_pre_ffn, g_post_ffn, w_in, w_gate, b_gate, w_o, conv_ssd_w, conv_ssd_b, dt_bias, a_log, d_skip, g_ssd_norm, w_ssd_out, g_q_norm, w_q_up, g_kv_norm, w_kv_up, w_mla_out, w_fft_out, w_up, conv_ffn_w, conv_ffn_b, w_down):
    batch, seq, _ = x_prompt.shape
    dec_batch, dec_seq, _ = x_sample.shape

    cvec = jnp.concatenate([c_ctx[None, :], c, jnp.zeros((SUBLANES - 1 - dec_batch, D_MODEL), F32)], axis=0)
    mods = _ada_params(cvec, w_ada, b_ada)

    vec = lambda v: v[:, None, :]
    pad_lane = lambda v: jnp.pad(v, ((0, 0), (0, 0), (0, LANES - v.shape[-1])))
    pw = {
        "g_pre_mix": vec(g_pre_mix), "g_post_mix": vec(g_post_mix),
        "g_pre_ffn": vec(g_pre_ffn), "g_post_ffn": vec(g_post_ffn),
        "w_in": w_in.astype(BF16),
        "w_gate": w_gate.astype(BF16), "b_gate": vec(b_gate),
        "w_o": w_o.astype(BF16), "w_ssd_out": w_ssd_out.astype(BF16),
        "w_mla_out": w_mla_out.astype(BF16), "w_fft_out": w_fft_out.astype(BF16),
        "conv_ssd_w": conv_ssd_w, "conv_ssd_b": vec(conv_ssd_b),
        "dt_bias": pad_lane(dt_bias.reshape(DEPTH, 1, 2 * SSD_HEADS)),
        "a_log": pad_lane(a_log.reshape(DEPTH, 1, 2 * SSD_HEADS)),
        "d_skip": vec(jnp.repeat(d_skip, SSD_HEAD_DIM, axis=1)),
        "g_ssd_norm": vec(g_ssd_norm),
        "g_q_norm": vec(g_q_norm), "g_kv_norm": vec(g_kv_norm),
        "w_q_pad": _pad_q_up(w_q_up), "w_kv_up": w_kv_up.astype(BF16),
        "w_up": w_up.astype(BF16), "conv_ffn_w": conv_ffn_w, "conv_ffn_b": vec(conv_ffn_b),
        "w_down": w_down.astype(BF16),
    }

    cs, dl_ctx = _dft_tables(seq)
    _, dl_lat = _dft_tables(dec_seq)
    tabs_ctx = {"rope": (_ctx_q_table(),), "cs": cs, "dl": dl_ctx}
    tabs_lat = {"rope": _rope_tables(dec_seq), "cs": cs, "dl": dl_lat}
    cache_kpe_p = jnp.pad(cache_kpe, ((0, 0), (0, 0), (0, 0), (0, LANES - ROPE_DIM)))
    h0_all = state_ssm.reshape(dec_batch, DEPTH, 2, D_INNER, SSD_STATE)
    ctx = (cache_ckv, cache_kpe_p, h0_all)

    y_p = x_prompt.reshape(batch * seq, D_MODEL)
    y_s = x_sample.reshape(dec_batch * dec_seq, D_MODEL)
    outs = None
    for i in range(DEPTH):
        y_p, outs = _trunk_pass(y_p, mods, pw, i, None, outs, tabs_ctx, batch, seq)
        y_s, _ = _trunk_pass(y_s, mods, pw, i, ctx, None, tabs_lat, dec_batch, dec_seq)
    new_ckv, new_kpe, new_ssm = outs
    return (y_p.reshape(batch, seq, D_MODEL), y_s.reshape(dec_batch, dec_seq, D_MODEL),
            new_ckv, new_kpe, new_ssm.reshape(batch, DEPTH, 2, SSD_HEADS, SSD_HEAD_DIM, SSD_STATE))
```

```python
import functools
import math

import numpy as np
import jax
import jax.numpy as jnp
from jax import lax
from jax.experimental import pallas as pl
from jax.experimental.pallas import tpu as pltpu

F32 = jnp.float32
BF16 = jnp.bfloat16

D_MODEL = 1024
DEPTH = 4
GRID_W = 64
EPS = 1e-6
SSD_HEADS = 16
SSD_HEAD_DIM = 64
D_INNER = SSD_HEADS * SSD_HEAD_DIM
SSD_GROUPS = 2
SSD_STATE = 128
SSD_CHUNK = 128
CONV_CH = D_INNER + 2 * SSD_GROUPS * SSD_STATE
MLA_HEADS = 16
Q_LORA = 384
KV_LORA = 256
QK_NOPE = 64
ROPE_DIM = 32
V_DIM = 64
ROPE_THETA = 10000.0
ATTN_SCALE = (QK_NOPE + ROPE_DIM) ** -0.5
FFT_GROUPS = 4
FFT_GROUP_W = 256
D_FF = 2816
N_BRANCH = 3
N_MOD = 6
OFF_XBC = D_INNER
OFF_DT = OFF_XBC + CONV_CH
OFF_QD = OFF_DT + 2 * SSD_HEADS
OFF_KVD = OFF_QD + Q_LORA
OFF_FFT = OFF_KVD + KV_LORA + ROPE_DIM

LANES = 128
SUBLANES = 8
HEAD_SLOT = LANES
KV_PAD = KV_LORA + LANES
DT_PAD = LANES

ROW_TILE = 256
FFT_ROWS = 1024
MERGE_ROWS = 512
FFN_ROWS = 1024
FFN_BLOCK = 256
FFN_NBLK = D_FF // FFN_BLOCK
Q_BLOCK_ROWS = 256
VMEM_LIMIT = 56 * 1024 * 1024
NEG_BIG = -1e30
LOG2E = math.log2(math.e)


def _params(sem, vmem=VMEM_LIMIT):
    return pltpu.CompilerParams(dimension_semantics=sem, vmem_limit_bytes=vmem)


def _sigmoid(x):
    return 1.0 / (1.0 + jnp.exp(-x))


def _silu(x):
    return x * _sigmoid(x)


def _rms(x, g):
    return x * lax.rsqrt(jnp.mean(x * x, axis=-1, keepdims=True) + EPS) * g


def _dot(a, b):
    return jnp.dot(a, b, preferred_element_type=F32)


def _dot_nt(a, b):
    return lax.dot_general(a, b, (((1,), (1,)), ((), ())), preferred_element_type=F32)


def _const_spec(shape):
    nd = len(shape)
    return pl.BlockSpec(shape, lambda *_: (0,) * nd, pipeline_mode=pl.Buffered(1))


def _layer_spec(arr, layer):
    shape = (1,) + tuple(arr.shape[1:])
    nd = len(shape)
    return pl.BlockSpec(shape, lambda *_: (layer,) + (0,) * (nd - 1), pipeline_mode=pl.Buffered(1))


def _mod_spec(layer, k):
    return pl.BlockSpec((1, 1, SUBLANES, D_MODEL), lambda *_: (layer, k, 0, 0), pipeline_mode=pl.Buffered(1))


def _mod_row(ref, row):
    return ref[0, 0, pl.ds(row, 1), :]


def _ada_kernel(c_ref, w_ref, b_ref, o_ref):
    c = c_ref[...]
    s = _silu(c).astype(BF16)
    o_ref[0, 0] = _dot(s, w_ref[0].astype(BF16)) + b_ref[0]


def _ada_params(cvec, w_ada, b_ada):
    return pl.pallas_call(
        _ada_kernel,
        grid=(DEPTH, N_MOD),
        in_specs=[
            pl.BlockSpec((SUBLANES, D_MODEL), lambda l, j: (0, 0)),
            pl.BlockSpec((1, D_MODEL, D_MODEL), lambda l, j: (l, 0, j)),
            pl.BlockSpec((1, 1, D_MODEL), lambda l, j: (l, 0, j)),
        ],
        out_specs=pl.BlockSpec((1, 1, SUBLANES, D_MODEL), lambda l, j: (l, j, 0, 0)),
        out_shape=jax.ShapeDtypeStruct((DEPTH, N_MOD, SUBLANES, D_MODEL), F32),
        compiler_params=_params(("parallel", "parallel")),
        name="ada_params",
    )(cvec, w_ada, b_ada.reshape(DEPTH, 1, N_MOD * D_MODEL))


def _pre_kernel(x_ref, sh_ref, sc_ref, g_ref, wi_ref, wg_ref, bg_ref,
                z_ref, xbc_ref, dt_ref, qd_ref, kvd_ref, fin_ref, gate_ref, *, mod_row):
    row = mod_row(pl.program_id(0))
    x = x_ref[...]
    h = (_rms(x, g_ref[0]) * (1.0 + _mod_row(sc_ref, row)) + _mod_row(sh_ref, row)).astype(BF16)

    def seg(out_ref, w_ref, start, width, bias_ref=None):
        step = 512
        for c0 in range(0, width, step):
            w = min(step, width - c0)
            r = _dot(h, w_ref[0, :, start + c0:start + c0 + w])
            if bias_ref is not None:
                r = _sigmoid(r + bias_ref[0, :, c0:c0 + w])
            out_ref[:, c0:c0 + w] = r.astype(out_ref.dtype)

    seg(z_ref, wi_ref, 0, D_INNER)
    seg(xbc_ref, wi_ref, OFF_XBC, CONV_CH)
    t = _dot(h, wi_ref[0, :, OFF_DT:])
    dt_ref[...] = t[:, 0:DT_PAD]
    qd_ref[...] = t[:, OFF_QD - OFF_DT:OFF_KVD - OFF_DT]
    kvd_ref[...] = t[:, OFF_KVD - OFF_DT:OFF_KVD - OFF_DT + KV_PAD]
    fin_ref[...] = t[:, OFF_FFT - OFF_DT:].astype(fin_ref.dtype)
    seg(gate_ref, wg_ref, 0, N_BRANCH * D_MODEL, bg_ref)


def _row_of_tile(rows_per_batch, latent):
    if latent:
        return lambda i: 1 + i // rows_per_batch
    return lambda i: 0


def _pre_mix(x, mods, pw, layer, seq, latent):
    t = x.shape[0]

    def row_spec(width):
        return pl.BlockSpec((ROW_TILE, width), lambda i: (i, 0))

    widths = (D_INNER, CONV_CH, DT_PAD, Q_LORA, KV_PAD, D_MODEL, N_BRANCH * D_MODEL)
    dtypes = (F32, F32, F32, F32, F32, BF16, F32)
    consts = [pw["g_pre_mix"], pw["w_in"], pw["w_gate"], pw["b_gate"]]
    return pl.pallas_call(
        functools.partial(_pre_kernel, mod_row=_row_of_tile(seq // ROW_TILE, latent)),
        grid=(t // ROW_TILE,),
        in_specs=[row_spec(D_MODEL), _mod_spec(layer, 0), _mod_spec(layer, 1)]
        + [_layer_spec(c, layer) for c in consts],
        out_specs=[row_spec(w) for w in widths],
        out_shape=[jax.ShapeDtypeStruct((t, w), d) for w, d in zip(widths, dtypes)],
        compiler_params=_params(("parallel",)),
        name="pre_mix",
    )(x, mods, mods, *consts)


def _split3(x):
    hi = x.astype(BF16)
    r1 = x - hi.astype(F32)
    mid = r1.astype(BF16)
    lo = (r1 - mid.astype(F32)).astype(BF16)
    return hi, mid, lo


def _ssd_kernel(*refs, seq, has_h0, want_state, chain_state, layer):
    it = iter(refs)
    z_ref, xbc_ref, dt_ref = next(it), next(it), next(it)
    h0_ref = next(it) if has_h0 else None
    cw_ref, cb_ref, dtb_ref, alog_ref, dsk_ref, gn_ref = (next(it) for _ in range(6))
    if chain_state:
        next(it)
    y_ref = next(it)
    hf_ref = next(it) if want_state else None
    bc_s, xs_s, yf_s, yb_s, ld_s, da_s, h_s = (next(it) for _ in range(7))
    lhs_a, lhs_b, bwt_a, bwt_b, cd_a, cd_b = (next(it) for _ in range(6))

    nc = seq // SSD_CHUNK
    rb = SSD_CHUNK

    cstep = 512
    for r0 in range(0, seq, rb):
        for c0 in range(0, CONV_CH, cstep):
            cs = slice(c0, c0 + cstep)
            cur = xbc_ref[r0:r0 + rb, cs]
            rid = lax.broadcasted_iota(jnp.int32, (rb, cstep), 0)
            prev = pltpu.roll(cur, 1, 0)
            if r0 > 0:
                prev = jnp.where(rid == 0, xbc_ref[r0 - 1:r0, cs], prev)
            else:
                prev = jnp.where(rid == 0, 0.0, prev)
            nxt = pltpu.roll(cur, rb - 1, 0)
            if r0 + rb < seq:
                nxt = jnp.where(rid == rb - 1, xbc_ref[r0 + rb:r0 + rb + 1, cs], nxt)
            else:
                nxt = jnp.where(rid == rb - 1, 0.0, nxt)
            u = (prev * cw_ref[0, 0:1, cs] + cur * cw_ref[0, 1:2, cs] + nxt * cw_ref[0, 2:3, cs]
                 + cb_ref[0, :, cs])
            u = _silu(u)
            if c0 < D_INNER:
                xs_s[r0:r0 + rb, cs] = u.astype(BF16)
                yf_s[r0:r0 + rb, cs] = u * dsk_ref[0, :, cs]
            else:
                bc_s[r0:r0 + rb, c0 - D_INNER:c0 - D_INNER + cstep] = u

    a_row = -jnp.exp(alog_ref[0]) * LOG2E
    for r0 in range(0, seq, rb):
        v = dt_ref[r0:r0 + rb, :] + dtb_ref[0]
        dtv = jnp.maximum(v, 0.0) + jnp.log1p(jnp.exp(-jnp.abs(v)))
        ld_s[r0:r0 + rb, :] = jnp.log2(dtv)
        da_s[r0:r0 + rb, :] = dtv * a_row

    for d in range(2):
        if has_h0:
            for hp in range(SSD_HEADS // 2):
                blk = h0_ref[0, 0, d, hp * LANES:(hp + 1) * LANES, :]
                h_s[d, :, hp * LANES:(hp + 1) * LANES] = blk.T
        else:
            h_s[d] = jnp.zeros((SSD_STATE, D_INNER), F32)

    ri = lax.broadcasted_iota(jnp.int32, (rb, rb), 0)
    ci = lax.broadcasted_iota(jnp.int32, (rb, rb), 1)
    lower = ri >= ci
    upper = ri <= ci
    tri_f = jnp.where(lower, 1.0, 0.0).astype(BF16)
    tri_b = jnp.where(upper, 1.0, 0.0).astype(BF16)
    first_half = ci < SSD_HEAD_DIM

    def rows_of(step, d):
        c = step if d == 0 else nc - 1 - step
        return c * rb if isinstance(c, int) else pl.multiple_of(c * rb, rb)

    def build(step, stage):
        lhs_ref, bwt_ref, cd_ref = stage
        for d in range(2):
            r = rows_of(step, d)
            mask = lower if d == 0 else upper
            tri = tri_f if d == 0 else tri_b
            end = rb - 1 if d == 0 else 0
            hi, mid, lo = _split3(da_s[pl.ds(r, rb), :])
            acum = _dot(tri, hi) + _dot(tri, mid) + _dot(tri, lo)
            acum_t = acum.T
            rl_t = acum_t - ld_s[pl.ds(r, rb), :].T
            w_t = jnp.exp2(acum_t[:, end:end + 1] - rl_t)
            for g in range(SSD_GROUPS):
                b_g = bc_s[pl.ds(r, rb), g * SSD_STATE:(g + 1) * SSD_STATE]
                c_g = bc_s[pl.ds(r, rb), (SSD_GROUPS + g) * SSD_STATE:(SSD_GROUPS + g + 1) * SSD_STATE]
                cb = _dot_nt(c_g.astype(BF16), b_g.astype(BF16))
                b_t = b_g.T
                for e in range(g * 8, g * 8 + 8):
                    k = d * SSD_HEADS + e
                    colb = jnp.broadcast_to(acum[:, k:k + 1], (rb, rb))
                    m_e = jnp.exp2(jnp.where(mask, colb - rl_t[k:k + 1, :], NEG_BIG)) * cb
                    e_col = jnp.exp2(colb)
                    lhs_ref[d, e] = jnp.concatenate([m_e.astype(BF16), (c_g * e_col).astype(BF16)], axis=1)
                    bwt_ref[d, e] = (b_t * w_t[k:k + 1, :]).astype(BF16)
                    cd_ref[d, e] = e_col[end:end + 1, :]

    def apply(step, stage):
        lhs_ref, bwt_ref, cd_ref = stage
        for d in range(2):
            r = rows_of(step, d)
            for hp in range(SSD_HEADS // 2):
                ls = slice(hp * LANES, (hp + 1) * LANES)
                x_pair = xs_s[pl.ds(r, rb), ls]
                h_pair = h_s[d, :, ls]
                rhs = jnp.concatenate([x_pair, h_pair.astype(BF16)], axis=0)
                y0 = _dot(lhs_ref[d, 2 * hp], rhs)
                y1 = _dot(lhs_ref[d, 2 * hp + 1], rhs)
                s0 = _dot(bwt_ref[d, 2 * hp], x_pair)
                s1 = _dot(bwt_ref[d, 2 * hp + 1], x_pair)
                y_pair = jnp.where(first_half, y0, y1)
                if d == 0:
                    yf_s[pl.ds(r, rb), ls] = yf_s[pl.ds(r, rb), ls] + y_pair
                else:
                    yb_s[pl.ds(r, rb), ls] = y_pair
                cd = jnp.where(first_half[0:1, :], cd_ref[d, 2 * hp], cd_ref[d, 2 * hp + 1])
                h_s[d, :, ls] = h_pair * cd + jnp.where(first_half, s0, s1)

    stage_a = (lhs_a, bwt_a, cd_a)
    stage_b = (lhs_b, bwt_b, cd_b)
    build(0, stage_a)

    def body(j, carry):
        t = 2 * j
        build(t + 1, stage_b)
        apply(t, stage_a)
        build(t + 2, stage_a)
        apply(t + 1, stage_b)
        return carry

    n_loop = (nc - 2) // 2
    if n_loop > 0:
        lax.fori_loop(0, n_loop, body, 0)
    build(nc - 1, stage_b)
    apply(nc - 2, stage_a)
    apply(nc - 1, stage_b)

    for r0 in range(0, seq, rb):
        y = (yf_s[r0:r0 + rb, :] + yb_s[r0:r0 + rb, :]) * _silu(z_ref[r0:r0 + rb, :])
        y_ref[r0:r0 + rb, :] = _rms(y, gn_ref[0]).astype(y_ref.dtype)

    if want_state:
        for d in range(2):
            for hp in range(SSD_HEADS // 2):
                slot = 0 if chain_state else layer
                hf_ref[0, slot, d, hp * LANES:(hp + 1) * LANES, :] = h_s[d, :, hp * LANES:(hp + 1) * LANES].T
        if not chain_state:
            for other in range(DEPTH):
                if other != layer:
                    hf_ref[0, other] = jnp.zeros((2, D_INNER, SSD_STATE), F32)


def _ssd_branch(z, xbc, dt, h0, state_buf, pw, layer, batch, seq, want_state):
    has_h0 = h0 is not None
    chain_state = state_buf is not None
    row = lambda w: pl.BlockSpec((seq, w), lambda b: (b, 0))
    state_spec = pl.BlockSpec((1, 1, 2, D_INNER, SSD_STATE), lambda b: (b, layer, 0, 0, 0))
    in_specs = [row(D_INNER), row(CONV_CH), row(DT_PAD)]
    args = [z, xbc, dt]
    if has_h0:
        in_specs.append(state_spec)
        args.append(h0)
    consts = [pw["conv_ssd_w"], pw["conv_ssd_b"], pw["dt_bias"], pw["a_log"], pw["d_skip"], pw["g_ssd_norm"]]
    in_specs += [_layer_spec(c, layer) for c in consts]
    args += consts
    aliases = {}
    if chain_state:
        aliases = {len(args): 1}
        in_specs.append(pl.BlockSpec(memory_space=pl.ANY))
        args.append(state_buf)
    out_specs = [row(D_INNER)]
    out_shape = [jax.ShapeDtypeStruct((batch * seq, D_INNER), BF16)]
    if want_state:
        out_specs.append(state_spec if chain_state else
                         pl.BlockSpec((1, DEPTH, 2, D_INNER, SSD_STATE), lambda b: (b, 0, 0, 0, 0)))
        out_shape.append(jax.ShapeDtypeStruct((batch, DEPTH, 2, D_INNER, SSD_STATE), F32))
    res = pl.pallas_call(
        functools.partial(_ssd_kernel, seq=seq, has_h0=has_h0, want_state=want_state,
                          chain_state=chain_state, layer=layer),
        grid=(batch,),
        in_specs=in_specs,
        out_specs=out_specs,
        out_shape=out_shape,
        input_output_aliases=aliases,
        scratch_shapes=[
            pltpu.VMEM((seq, CONV_CH - D_INNER), F32),
            pltpu.VMEM((seq, D_INNER), BF16),
            pltpu.VMEM((seq, D_INNER), F32),
            pltpu.VMEM((seq, D_INNER), F32),
            pltpu.VMEM((seq, DT_PAD), F32),
            pltpu.VMEM((seq, DT_PAD), F32),
            pltpu.VMEM((2, SSD_STATE, D_INNER), F32),
        ] + 2 * [pltpu.VMEM((2, SSD_HEADS, SSD_CHUNK, 2 * SSD_CHUNK), BF16)]
          + 2 * [pltpu.VMEM((2, SSD_HEADS, SSD_STATE, SSD_CHUNK), BF16)]
          + 2 * [pltpu.VMEM((2, SSD_HEADS, 1, SSD_STATE), F32)],
        compiler_params=_params(("parallel",)),
        name="ssd_branch",
    )(*args)
    return (res[0], res[1]) if want_state else (res[0], None)


def _mla_kernel(*refs, seq, n_cache, chain_kv, layer):
    it = iter(refs)
    qd_ref, kvd_ref = next(it), next(it)
    latent = n_cache > 0
    qtab_ref = next(it)
    if latent:
        cckv_ref, ckpe_ref = next(it), next(it)
        kcos_ref, ksn_ref, ksp_ref = (next(it) for _ in range(3))
    gq_ref, gkv_ref, wq_ref, wkv_ref = (next(it) for _ in range(4))
    if chain_kv:
        next(it), next(it)
    o_ref = next(it)
    if not latent:
        ckv_out_ref, kpe_out_ref = next(it), next(it)
    q_s, k_s, v_s = (next(it) for _ in range(3))
    if latent:
        o_s, s_a, s_b, m_a, m_b, p_a, p_b = (next(it) for _ in range(7))

    rb = min(Q_BLOCK_ROWS, seq)
    n_pairs = MLA_HEADS // 2
    lane = lax.broadcasted_iota(jnp.int32, (rb, LANES), 1)
    first_half = lane < V_DIM
    ones_even = jnp.where(lane == V_DIM, 1.0, 0.0)
    ones_odd = jnp.where(lane == 0, 1.0, 0.0)

    def pair_cols(i):
        return slice(2 * i * HEAD_SLOT, 2 * (i + 1) * HEAD_SLOT)

    for r0 in range(0, seq, rb):
        qn = _rms(qd_ref[r0:r0 + rb, :], gq_ref[0]).astype(BF16)
        tab = qtab_ref[r0:r0 + rb, :] if latent else qtab_ref[...]
        for hp in range(n_pairs):
            qq = _dot(qn, wq_ref[0, :, pair_cols(hp)])
            q_s[2 * hp, r0:r0 + rb, :] = (qq[:, :HEAD_SLOT] * tab).astype(BF16)
            q_s[2 * hp + 1, r0:r0 + rb, :] = (qq[:, HEAD_SLOT:] * tab).astype(BF16)

    def put_keys(row0, ckv_n, kpe_tile):
        kpe_slot = pltpu.roll(kpe_tile, QK_NOPE, 1) + pltpu.roll(kpe_tile, QK_NOPE + ROPE_DIM, 1)
        cb16 = ckv_n.astype(BF16)
        for hp in range(n_pairs):
            kv = _dot(cb16, wkv_ref[0, :, pair_cols(hp)])
            kv_e, kv_o = kv[:, :HEAD_SLOT], kv[:, HEAD_SLOT:]
            k_s[2 * hp, row0:row0 + rb, :] = (jnp.where(first_half, kv_e, 0.0) + kpe_slot).astype(BF16)
            k_s[2 * hp + 1, row0:row0 + rb, :] = (jnp.where(first_half, kv_o, 0.0) + kpe_slot).astype(BF16)
            v_e = pltpu.roll(kv_e, V_DIM, 1)
            if latent:
                v_s[2 * hp, row0:row0 + rb, :] = (jnp.where(first_half, v_e, 0.0) + ones_even).astype(BF16)
                v_s[2 * hp + 1, row0:row0 + rb, :] = (jnp.where(first_half, 0.0, kv_o) + ones_odd).astype(BF16)
            else:
                v_s[hp, row0:row0 + rb, :] = jnp.where(first_half, v_e, kv_o).astype(BF16)

    for r0 in range(0, seq, rb):
        ckv_n = _rms(kvd_ref[r0:r0 + rb, 0:KV_LORA], gkv_ref[0])
        kpe = jnp.where(lane < ROPE_DIM, kvd_ref[r0:r0 + rb, KV_LORA:KV_PAD], 0.0)
        if latent:
            kpe = (kpe * kcos_ref[r0:r0 + rb, :]
                   + pltpu.roll(kpe, LANES - 8, 1) * ksn_ref[r0:r0 + rb, :]
                   + pltpu.roll(kpe, 8, 1) * ksp_ref[r0:r0 + rb, :])
        else:
            slot = 0 if chain_kv else layer
            ckv_out_ref[0, slot, r0:r0 + rb, :] = ckv_n
            kpe_out_ref[0, slot, r0:r0 + rb, :] = kpe[:, 0:ROPE_DIM]
            if not chain_kv:
                for other in range(DEPTH):
                    if other != layer:
                        ckv_out_ref[0, other, r0:r0 + rb, :] = jnp.zeros((rb, KV_LORA), F32)
                        kpe_out_ref[0, other, r0:r0 + rb, :] = jnp.zeros((rb, ROPE_DIM), F32)
        put_keys(r0, ckv_n, kpe)
    for r0 in range(0, n_cache, rb):
        put_keys(seq + r0, cckv_ref[0, 0, r0:r0 + rb, :], ckpe_ref[0, 0, r0:r0 + rb, :])

    def normalise(o0, o1):
        inv0 = 1.0 / o0[:, V_DIM:V_DIM + 1]
        inv1 = 1.0 / o1[:, 0:1]
        return jnp.where(first_half, o0 * inv0, o1 * inv1).astype(BF16)

    if not latent:
        for hp in range(n_pairs):
            outs = []
            for j in range(2):
                h = 2 * hp + j
                s = _dot_nt(q_s[h], k_s[h])
                p = jnp.exp2(s - jnp.max(s, axis=-1, keepdims=True))
                l = jnp.sum(p, axis=-1, keepdims=True)
                outs.append(_dot(p.astype(BF16), v_s[hp]) * (1.0 / l))
            o_ref[:, hp * LANES:(hp + 1) * LANES] = jnp.where(first_half, outs[0], outs[1]).astype(BF16)
        return

    nqb = seq // rb
    n_units = n_pairs * nqb

    def unit(u):
        if isinstance(u, int):
            return u // nqb, (u % nqb) * rb
        return u // nqb, pl.multiple_of((u % nqb) * rb, rb)

    def scores(u, s_ref, m_ref):
        hp, r = unit(u)
        for j in range(2):
            s = _dot_nt(q_s[2 * hp + j, pl.ds(r, rb), :], k_s[2 * hp + j])
            s_ref[j] = s
            m_ref[j] = jnp.max(s, axis=-1, keepdims=True)

    def probs(s_ref, m_ref, p_ref):
        for j in range(2):
            p_ref[j] = jnp.exp2(s_ref[j] - m_ref[j]).astype(BF16)

    def values(u, p_ref):
        hp, r = unit(u)
        o_s[hp, pl.ds(r, rb), :] = normalise(_dot(p_ref[0], v_s[2 * hp]), _dot(p_ref[1], v_s[2 * hp + 1]))

    scores(0, s_a, m_a)
    scores(1, s_b, m_b)
    probs(s_a, m_a, p_a)

    def body(j, carry):
        u = 2 * j
        scores(u, s_a, m_a)
        probs(s_b, m_b, p_b)
        values(u - 2, p_a)
        scores(u + 1, s_b, m_b)
        probs(s_a, m_a, p_a)
        values(u - 1, p_b)
        return carry

    lax.fori_loop(1, n_units // 2, body, 0)
    probs(s_b, m_b, p_b)
    values(n_units - 2, p_a)
    values(n_units - 1, p_b)

    for hp in range(n_pairs):
        o_ref[:, hp * LANES:(hp + 1) * LANES] = o_s[hp]


def _mla_branch(qd, kvd, cache, rope_tabs, kv_bufs, pw, layer, batch, seq):
    latent = cache is not None
    chain_kv = kv_bufs is not None
    n_cache = cache[0].shape[2] if latent else 0
    row = lambda w: pl.BlockSpec((seq, w), lambda b: (b, 0))
    in_specs = [row(Q_LORA), row(KV_PAD), _const_spec(rope_tabs[0].shape)]
    args = [qd, kvd, rope_tabs[0]]
    if latent:
        in_specs += [pl.BlockSpec((1, 1, n_cache, KV_LORA), lambda b: (b, layer, 0, 0)),
                     pl.BlockSpec((1, 1, n_cache, LANES), lambda b: (b, layer, 0, 0))]
        args += list(cache)
        in_specs += [_const_spec((seq, LANES))] * 3
        args += list(rope_tabs[1:])
    consts = [pw["g_q_norm"], pw["g_kv_norm"], pw["w_q_pad"], pw["w_kv_up"]]
    in_specs += [_layer_spec(c, layer) for c in consts]
    args += consts
    aliases = {}
    if chain_kv:
        aliases = {len(args): 1, len(args) + 1: 2}
        in_specs += [pl.BlockSpec(memory_space=pl.ANY)] * 2
        args += list(kv_bufs)
    out_specs = [row(D_MODEL)]
    out_shape = [jax.ShapeDtypeStruct((batch * seq, D_MODEL), BF16)]
    if not latent:
        n_slot, first = (1, layer) if chain_kv else (DEPTH, 0)
        out_specs += [pl.BlockSpec((1, n_slot, seq, KV_LORA), lambda b: (b, first, 0, 0)),
                      pl.BlockSpec((1, n_slot, seq, ROPE_DIM), lambda b: (b, first, 0, 0))]
        out_shape += [jax.ShapeDtypeStruct((batch, DEPTH, seq, KV_LORA), F32),
                      jax.ShapeDtypeStruct((batch, DEPTH, seq, ROPE_DIM), F32)]
    n_keys = seq + n_cache
    scratch = [
        pltpu.VMEM((MLA_HEADS, seq, HEAD_SLOT), BF16),
        pltpu.VMEM((MLA_HEADS, n_keys, HEAD_SLOT), BF16),
        pltpu.VMEM((MLA_HEADS if latent else MLA_HEADS // 2, n_keys, LANES), BF16),
    ]
    if latent:
        tile = (2, min(Q_BLOCK_ROWS, seq), n_keys)
        stat = (2, min(Q_BLOCK_ROWS, seq), 1)
        scratch += [pltpu.VMEM((MLA_HEADS // 2, seq, LANES), BF16),
                    pltpu.VMEM(tile, F32), pltpu.VMEM(tile, F32),
                    pltpu.VMEM(stat, F32), pltpu.VMEM(stat, F32),
                    pltpu.VMEM(tile, BF16), pltpu.VMEM(tile, BF16)]
    return pl.pallas_call(
        functools.partial(_mla_kernel, seq=seq, n_cache=n_cache, chain_kv=chain_kv, layer=layer),
        grid=(batch,),
        in_specs=in_specs,
        out_specs=out_specs,
        out_shape=out_shape,
        input_output_aliases=aliases,
        scratch_shapes=scratch,
        compiler_params=_params(("parallel",)),
        name="mla_branch",
    )(*args)


def _fft_kernel(f_ref, cs_ref, dl_ref, o_ref, *, seq, n_seq):
    scale = 1.0 / math.sqrt(seq * FFT_GROUP_W)
    for s in range(n_seq):
        rs = slice(s * seq, (s + 1) * seq)
        for g in range(FFT_GROUPS):
            gs = slice(g * FFT_GROUP_W, (g + 1) * FFT_GROUP_W)
            t = _dot(f_ref[rs, gs], cs_ref[...]).astype(BF16)
            stack = jnp.concatenate([t[:, :FFT_GROUP_W], t[:, FFT_GROUP_W:]], axis=0)
            o_ref[rs, gs] = (_dot(dl_ref[...], stack) * scale).astype(o_ref.dtype)


def _fft_branch(f_in, cs, dl, batch, seq):
    n_seq = max(FFT_ROWS // seq, 1)
    row = pl.BlockSpec((n_seq * seq, D_MODEL), lambda b: (b, 0))
    return pl.pallas_call(
        functools.partial(_fft_kernel, seq=seq, n_seq=n_seq),
        grid=(batch // n_seq,),
        in_specs=[row, _const_spec(cs.shape), _const_spec(dl.shape)],
        out_specs=row,
        out_shape=jax.ShapeDtypeStruct((batch * seq, D_MODEL), BF16),
        compiler_params=_params(("parallel",)),
        name="fft_branch",
    )(f_in, cs, dl)


def _dft_tables(seq):
    def cos_sin(n):
        idx = np.arange(n, dtype=np.int64)
        ang = 2.0 * np.pi * ((idx[:, None] * idx[None, :]) % n) / n
        return np.cos(ang), np.sin(ang)

    cc, sc = cos_sin(FFT_GROUP_W)
    cl, sl = cos_sin(seq)
    cs = np.concatenate([cc, sc], axis=1).astype(np.float32)
    dl = np.concatenate([cl, -sl], axis=1).astype(np.float32)
    return jnp.asarray(cs).astype(BF16), jnp.asarray(dl).astype(BF16)


def _merge_kernel(y_ref, o_ref, f_ref, gate_ref, x_ref, g1_ref, gp_ref,
                  ws_ref, wm_ref, wf_ref, wo_ref, out_ref, *, mod_row):
    row = mod_row(pl.program_id(0))
    mix = (gate_ref[:, 0:D_MODEL] * _dot(y_ref[...], ws_ref[0])
           + gate_ref[:, D_MODEL:2 * D_MODEL] * _dot(o_ref[...], wm_ref[0])
           + gate_ref[:, 2 * D_MODEL:3 * D_MODEL] * _dot(f_ref[...], wf_ref[0]))
    out = _dot(mix.astype(BF16), wo_ref[0])
    out_ref[...] = x_ref[...] + _mod_row(g1_ref, row) * _rms(out, gp_ref[0])


def _merge(y, o, f, gates, x, mods, pw, layer, seq, latent):
    t = x.shape[0]
    row = lambda w: pl.BlockSpec((MERGE_ROWS, w), lambda i: (i, 0))
    consts = [pw["g_post_mix"], pw["w_ssd_out"], pw["w_mla_out"], pw["w_fft_out"], pw["w_o"]]
    return pl.pallas_call(
        functools.partial(_merge_kernel, mod_row=_row_of_tile(max(seq // MERGE_ROWS, 1), latent)),
        grid=(t // MERGE_ROWS,),
        in_specs=[row(D_MODEL), row(D_MODEL), row(D_MODEL), row(N_BRANCH * D_MODEL), row(D_MODEL),
                  _mod_spec(layer, 2)] + [_layer_spec(c, layer) for c in consts],
        out_specs=row(D_MODEL),
        out_shape=jax.ShapeDtypeStruct((t, D_MODEL), F32),
        compiler_params=_params(("parallel",)),
        name="merge",
    )(y, o, f, gates, x, mods, *consts)


def _ffn_kernel(x_ref, sh_ref, sc_ref, g2_ref, gpre_ref, gpost_ref,
                wu_ref, cw_ref, cb_ref, wd_ref,
                out_ref, h_s, acc_s, u_a, u_b, *, seq, mod_row):
    row = mod_row(pl.program_id(0))
    h_s[...] = (_rms(x_ref[...], gpre_ref[0]) * (1.0 + _mod_row(sc_ref, row))
                + _mod_row(sh_ref, row)).astype(BF16)

    row8 = lax.broadcasted_iota(jnp.int32, (SUBLANES, FFN_BLOCK), 0)

    def neighbours(u):
        prev = pltpu.roll(u, 1, 0)
        nxt = pltpu.roll(u, FFN_ROWS - 1, 0)
        pp, nn = [], []
        for s0 in range(0, FFN_ROWS, seq):
            pp += [jnp.where(row8 == 0, 0.0, prev[s0:s0 + SUBLANES]), prev[s0 + SUBLANES:s0 + seq]]
            nn += [nxt[s0:s0 + seq - SUBLANES],
                   jnp.where(row8 == SUBLANES - 1, 0.0, nxt[s0 + seq - SUBLANES:s0 + seq])]
        return jnp.concatenate(pp, axis=0), jnp.concatenate(nn, axis=0)

    def cols(k, gate_half):
        c0 = (D_FF if gate_half else 0) + k * FFN_BLOCK
        if isinstance(k, int):
            return slice(c0, c0 + FFN_BLOCK)
        return pl.ds(pl.multiple_of(c0, FFN_BLOCK), FFN_BLOCK)

    def rows(k):
        if isinstance(k, int):
            return slice(k * FFN_BLOCK, (k + 1) * FFN_BLOCK)
        return pl.ds(pl.multiple_of(k * FFN_BLOCK, FFN_BLOCK), FFN_BLOCK)

    def up(k, u_ref):
        h = h_s[...]
        u_ref[0] = _dot(h, wu_ref[0, :, cols(k, False)])
        u_ref[1] = _dot(h, wu_ref[0, :, cols(k, True)])

    def conv(u, cs):
        prev, nxt = neighbours(u)
        return (prev * cw_ref[0, 0:1, cs] + u * cw_ref[0, 1:2, cs] + nxt * cw_ref[0, 2:3, cs]
                + cb_ref[0, :, cs])

    def down(k, u_ref, first=False):
        a = conv(u_ref[0], cols(k, False))
        g = conv(u_ref[1], cols(k, True))
        d = _dot((_silu(a) * g).astype(BF16), wd_ref[0, rows(k), :])
        if first:
            acc_s[...] = d
        else:
            acc_s[...] += d

    up(0, u_a)
    up(1, u_b)
    down(0, u_a, first=True)

    def body(i, carry):
        up(2 * i, u_a)
        down(2 * i - 1, u_b)
        up(2 * i + 1, u_b)
        down(2 * i, u_a)
        return carry

    n_pairs = (FFN_NBLK - 1) // 2
    lax.fori_loop(1, n_pairs, body, 0)
    last = 2 * n_pairs
    up(last, u_a)
    down(last - 1, u_b)
    down(last, u_a)

    out_ref[...] = x_ref[...] + _mod_row(g2_ref, row) * _rms(acc_s[...], gpost_ref[0])


def _ffn(x, mods, pw, layer, seq, latent):
    t = x.shape[0]
    row = pl.BlockSpec((FFN_ROWS, D_MODEL), lambda i: (i, 0))
    consts = [pw["g_pre_ffn"], pw["g_post_ffn"], pw["w_up"], pw["conv_ffn_w"], pw["conv_ffn_b"], pw["w_down"]]
    stage = pltpu.VMEM((2, FFN_ROWS, FFN_BLOCK), F32)
    return pl.pallas_call(
        functools.partial(_ffn_kernel, seq=seq, mod_row=_row_of_tile(max(seq // FFN_ROWS, 1), latent)),
        grid=(t // FFN_ROWS,),
        in_specs=[row, _mod_spec(layer, 3), _mod_spec(layer, 4), _mod_spec(layer, 5)]
        + [_layer_spec(c, layer) for c in consts],
        out_specs=row,
        out_shape=jax.ShapeDtypeStruct((t, D_MODEL), F32),
        scratch_shapes=[pltpu.VMEM((FFN_ROWS, D_MODEL), BF16), pltpu.VMEM((FFN_ROWS, D_MODEL), F32),
                        stage, stage],
        compiler_params=_params(("parallel",)),
        name="conv_mlp",
    )(x, mods, mods, mods, *consts)


def _rope_partner():
    idx = np.arange(ROPE_DIM)
    quarter = ROPE_DIM // 4
    return np.where(idx % (2 * quarter) < quarter, idx + quarter, idx - quarter)


def _pad_q_up(w_q_up):
    w = w_q_up.reshape(DEPTH, Q_LORA, MLA_HEADS, QK_NOPE + ROPE_DIM)
    pe = w[..., QK_NOPE:]
    w = jnp.concatenate([w, pe[..., _rope_partner()]], axis=-1)
    return w.reshape(DEPTH, Q_LORA, MLA_HEADS * HEAD_SLOT).astype(BF16)


def _rope_tables(seq):
    rows = seq // GRID_W
    row = jnp.repeat(jnp.arange(rows), GRID_W)
    col = jnp.tile(jnp.arange(GRID_W), rows)
    quarter = ROPE_DIM // 4
    inv = ROPE_THETA ** (-jnp.arange(quarter, dtype=F32) / quarter)
    ang_row = row.astype(F32)[:, None] * inv
    ang_col = col.astype(F32)[:, None] * inv
    cos = jnp.concatenate([jnp.cos(ang_row)] * 2 + [jnp.cos(ang_col)] * 2, axis=1)
    sin = jnp.concatenate([jnp.sin(ang_row)] * 2 + [jnp.sin(ang_col)] * 2, axis=1)
    lo = (jnp.arange(ROPE_DIM) % (2 * quarter)) < quarter
    sin_signed = jnp.where(lo, -sin, sin)
    scale = ATTN_SCALE * LOG2E
    q_tab = jnp.concatenate([jnp.full((seq, QK_NOPE), scale, F32), scale * cos, scale * sin_signed], axis=1)
    pad = lambda t, fill: jnp.concatenate([t, jnp.full((seq, LANES - ROPE_DIM), fill, F32)], axis=1)
    return (q_tab, pad(cos, 1.0), pad(jnp.where(lo, -sin, 0.0), 0.0), pad(jnp.where(lo, 0.0, sin), 0.0))


def _ctx_q_table():
    scale = ATTN_SCALE * LOG2E
    return jnp.concatenate([jnp.full((1, QK_NOPE + ROPE_DIM), scale, F32),
                            jnp.zeros((1, HEAD_SLOT - QK_NOPE - ROPE_DIM), F32)], axis=1)


def _trunk_pass(x, mods, pw, layer, ctx, outs, tabs, batch, seq):
    latent = ctx is not None
    z, xbc, dt, qd, kvd, f_in, gates = _pre_mix(x, mods, pw, layer, seq, latent)
    y_ssd, ssm = _ssd_branch(z, xbc, dt, ctx[2] if latent else None, outs[2] if outs else None,
                             pw, layer, batch, seq, want_state=not latent)
    mla = _mla_branch(qd, kvd, (ctx[0], ctx[1]) if latent else None, tabs["rope"],
                      (outs[0], outs[1]) if outs else None, pw, layer, batch, seq)
    f = _fft_branch(f_in, tabs["cs"], tabs["dl"], batch, seq)
    x1 = _merge(y_ssd, mla[0], f, gates, x, mods, pw, layer, seq, latent)
    x2 = _ffn(x1, mods, pw, layer, seq, latent)
    if latent:
        return x2, None
    return x2, (mla[1], mla[2], ssm)


def kernel(x_prompt, x_sample, cache_ckv, cache_kpe, state_ssm, c, c_ctx, w_ada, b_ada, g_pre_mix, g_post_mix, g_pre_ffn, g_post_ffn, w_in, w_gate, b_gate, w_o, conv_ssd_w, conv_ssd_b, dt_bias, a_log, d_skip, g_ssd_norm, w_ssd_out, g_q_norm, w_q_up, g_kv_norm, w_kv_up, w_mla_out, w_fft_out, w_up, conv_ffn_w, conv_ffn_b, w_down):
    batch, seq, _ = x_prompt.shape
    dec_batch, dec_seq, _ = x_sample.shape

    cvec = jnp.concatenate([c_ctx[None, :], c, jnp.zeros((SUBLANES - 1 - dec_batch, D_MODEL), F32)], axis=0)
    mods = _ada_params(cvec, w_ada, b_ada)

    vec = lambda v: v[:, None, :]
    pad_lane = lambda v: jnp.pad(v, ((0, 0), (0, 0), (0, LANES - v.shape[-1])))
    pw = {
        "g_pre_mix": vec(g_pre_mix), "g_post_mix": vec(g_post_mix),
        "g_pre_ffn": vec(g_pre_ffn), "g_post_ffn": vec(g_post_ffn),
        "w_in": w_in.astype(BF16),
        "w_gate": w_gate.astype(BF16), "b_gate": vec(b_gate),
        "w_o": w_o.astype(BF16), "w_ssd_out": w_ssd_out.astype(BF16),
        "w_mla_out": w_mla_out.astype(BF16), "w_fft_out": w_fft_out.astype(BF16),
        "conv_ssd_w": conv_ssd_w, "conv_ssd_b": vec(conv_ssd_b),
        "dt_bias": pad_lane(dt_bias.reshape(DEPTH, 1, 2 * SSD_HEADS)),
        "a_log": pad_lane(a_log.reshape(DEPTH, 1, 2 * SSD_HEADS)),
        "d_skip": vec(jnp.repeat(d_skip, SSD_HEAD_DIM, axis=1)),
        "g_ssd_norm": vec(g_ssd_norm),
        "g_q_norm": vec(g_q_norm), "g_kv_norm": vec(g_kv_norm),
        "w_q_pad": _pad_q_up(w_q_up), "w_kv_up": w_kv_up.astype(BF16),
        "w_up": w_up.astype(BF16), "conv_ffn_w": conv_ffn_w, "conv_ffn_b": vec(conv_ffn_b),
        "w_down": w_down.astype(BF16),
    }

    cs, dl_ctx = _dft_tables(seq)
    _, dl_lat = _dft_tables(dec_seq)
    tabs_ctx = {"rope": (_ctx_q_table(),), "cs": cs, "dl": dl_ctx}
    tabs_lat = {"rope": _rope_tables(dec_seq), "cs": cs, "dl": dl_lat}
    cache_kpe_p = jnp.pad(cache_kpe, ((0, 0), (0, 0), (0, 0), (0, LANES - ROPE_DIM)))
    h0_all = state_ssm.reshape(dec_batch, DEPTH, 2, D_INNER, SSD_STATE)
    ctx = (cache_ckv, cache_kpe_p, h0_all)

    y_p = x_prompt.reshape(batch * seq, D_MODEL)
    y_s = x_sample.reshape(dec_batch * dec_seq, D_MODEL)
    outs = None
    for i in range(DEPTH):
        y_p, outs = _trunk_pass(y_p, mods, pw, i, None, outs, tabs_ctx, batch, seq)
        y_s, _ = _trunk_pass(y_s, mods, pw, i, ctx, None, tabs_lat, dec_batch, dec_seq)
    new_ckv, new_kpe, new_ssm = outs
    return (y_p.reshape(batch, seq, D_MODEL), y_s.reshape(dec_batch, dec_seq, D_MODEL),
            new_ckv, new_kpe, new_ssm.reshape(batch, DEPTH, 2, SSD_HEADS, SSD_HEAD_DIM, SSD_STATE))
```

```python
import functools
import math

import numpy as np
import jax
import jax.numpy as jnp
from jax import lax
from jax.experimental import pallas as pl
from jax.experimental.pallas import tpu as pltpu

F32 = jnp.float32
BF16 = jnp.bfloat16

D_MODEL = 1024
DEPTH = 4
GRID_W = 64
EPS = 1e-6
SSD_HEADS = 16
SSD_HEAD_DIM = 64
D_INNER = SSD_HEADS * SSD_HEAD_DIM
SSD_GROUPS = 2
SSD_STATE = 128
SSD_CHUNK = 128
CONV_CH = D_INNER + 2 * SSD_GROUPS * SSD_STATE
MLA_HEADS = 16
Q_LORA = 384
KV_LORA = 256
QK_NOPE = 64
ROPE_DIM = 32
V_DIM = 64
ROPE_THETA = 10000.0
ATTN_SCALE = (QK_NOPE + ROPE_DIM) ** -0.5
FFT_GROUPS = 4
FFT_GROUP_W = 256
D_FF = 2816
N_BRANCH = 3
N_MOD = 6
OFF_XBC = D_INNER
OFF_DT = OFF_XBC + CONV_CH
OFF_QD = OFF_DT + 2 * SSD_HEADS
OFF_KVD = OFF_QD + Q_LORA
OFF_FFT = OFF_KVD + KV_LORA + ROPE_DIM

LANES = 128
SUBLANES = 8
HEAD_SLOT = LANES
KV_PAD = KV_LORA + LANES
DT_PAD = LANES

ROW_TILE = 256
FFT_ROWS = 1024
MERGE_ROWS = 512
FFN_ROWS = 1024
FFN_BLOCK = 256
FFN_NBLK = D_FF // FFN_BLOCK
Q_BLOCK_ROWS = 256
VMEM_LIMIT = 56 * 1024 * 1024
NEG_BIG = -1e30
LOG2E = math.log2(math.e)


def _params(sem, vmem=VMEM_LIMIT):
    return pltpu.CompilerParams(dimension_semantics=sem, vmem_limit_bytes=vmem)


def _sigmoid(x):
    return 1.0 / (1.0 + jnp.exp(-x))


def _silu(x):
    return x * _sigmoid(x)


def _rms(x, g):
    return x * lax.rsqrt(jnp.mean(x * x, axis=-1, keepdims=True) + EPS) * g


def _dot(a, b):
    return jnp.dot(a, b, preferred_element_type=F32)


def _dot_nt(a, b):
    return lax.dot_general(a, b, (((1,), (1,)), ((), ())), preferred_element_type=F32)


def _const_spec(shape):
    nd = len(shape)
    return pl.BlockSpec(shape, lambda *_: (0,) * nd, pipeline_mode=pl.Buffered(1))


def _layer_spec(arr, layer):
    shape = (1,) + tuple(arr.shape[1:])
    nd = len(shape)
    return pl.BlockSpec(shape, lambda *_: (layer,) + (0,) * (nd - 1), pipeline_mode=pl.Buffered(1))


def _mod_spec(layer, k):
    return pl.BlockSpec((1, 1, SUBLANES, D_MODEL), lambda *_: (layer, k, 0, 0), pipeline_mode=pl.Buffered(1))


def _mod_row(ref, row):
    return ref[0, 0, pl.ds(row, 1), :]


def _ada_kernel(c_ref, w_ref, b_ref, o_ref):
    c = c_ref[...]
    s = _silu(c).astype(BF16)
    o_ref[0, 0] = _dot(s, w_ref[0].astype(BF16)) + b_ref[0]


def _ada_params(cvec, w_ada, b_ada):
    return pl.pallas_call(
        _ada_kernel,
        grid=(DEPTH, N_MOD),
        in_specs=[
            pl.BlockSpec((SUBLANES, D_MODEL), lambda l, j: (0, 0)),
            pl.BlockSpec((1, D_MODEL, D_MODEL), lambda l, j: (l, 0, j)),
            pl.BlockSpec((1, 1, D_MODEL), lambda l, j: (l, 0, j)),
        ],
        out_specs=pl.BlockSpec((1, 1, SUBLANES, D_MODEL), lambda l, j: (l, j, 0, 0)),
        out_shape=jax.ShapeDtypeStruct((DEPTH, N_MOD, SUBLANES, D_MODEL), F32),
        compiler_params=_params(("parallel", "parallel")),
        name="ada_params",
    )(cvec, w_ada, b_ada.reshape(DEPTH, 1, N_MOD * D_MODEL))


def _pre_kernel(x_ref, sh_ref, sc_ref, g_ref, wi_ref, wg_ref, bg_ref,
                z_ref, xbc_ref, dt_ref, qd_ref, kvd_ref, fin_ref, gate_ref, *, mod_row):
    row = mod_row(pl.program_id(0))
    x = x_ref[...]
    h = (_rms(x, g_ref[0]) * (1.0 + _mod_row(sc_ref, row)) + _mod_row(sh_ref, row)).astype(BF16)

    def seg(out_ref, w_ref, start, width, bias_ref=None):
        step = 512
        for c0 in range(0, width, step):
            w = min(step, width - c0)
            if bias_ref is None:
                r = _dot_nt(h, w_ref[0, start + c0:start + c0 + w, :])
            else:
                r = _sigmoid(_dot(h, w_ref[0, :, start + c0:start + c0 + w]) + bias_ref[0, :, c0:c0 + w])
            out_ref[:, c0:c0 + w] = r.astype(out_ref.dtype)

    seg(z_ref, wi_ref, 0, D_INNER)
    seg(xbc_ref, wi_ref, OFF_XBC, CONV_CH)
    t = _dot_nt(h, wi_ref[0, OFF_DT:, :])
    dt_ref[...] = t[:, 0:DT_PAD]
    qd_ref[...] = t[:, OFF_QD - OFF_DT:OFF_KVD - OFF_DT]
    kvd_ref[...] = t[:, OFF_KVD - OFF_DT:OFF_KVD - OFF_DT + KV_PAD]
    fin_ref[...] = t[:, OFF_FFT - OFF_DT:].astype(fin_ref.dtype)
    seg(gate_ref, wg_ref, 0, N_BRANCH * D_MODEL, bg_ref)


def _row_of_tile(rows_per_batch, latent):
    if latent:
        return lambda i: 1 + i // rows_per_batch
    return lambda i: 0


def _pre_mix(x, mods, pw, layer, seq, latent):
    t = x.shape[0]

    def row_spec(width):
        return pl.BlockSpec((ROW_TILE, width), lambda i: (i, 0))

    widths = (D_INNER, CONV_CH, DT_PAD, Q_LORA, KV_PAD, D_MODEL, N_BRANCH * D_MODEL)
    dtypes = (F32, F32, F32, F32, F32, BF16, F32)
    consts = [pw["g_pre_mix"], pw["w_in"], pw["w_gate"], pw["b_gate"]]
    return pl.pallas_call(
        functools.partial(_pre_kernel, mod_row=_row_of_tile(seq // ROW_TILE, latent)),
        grid=(t // ROW_TILE,),
        in_specs=[row_spec(D_MODEL), _mod_spec(layer, 0), _mod_spec(layer, 1)]
        + [_layer_spec(c, layer) for c in consts],
        out_specs=[row_spec(w) for w in widths],
        out_shape=[jax.ShapeDtypeStruct((t, w), d) for w, d in zip(widths, dtypes)],
        compiler_params=_params(("parallel",)),
        name="pre_mix",
    )(x, mods, mods, *consts)


def _split3(x):
    hi = x.astype(BF16)
    r1 = x - hi.astype(F32)
    mid = r1.astype(BF16)
    lo = (r1 - mid.astype(F32)).astype(BF16)
    return hi, mid, lo


def _ssd_kernel(*refs, seq, has_h0, want_state, chain_state, layer):
    it = iter(refs)
    z_ref, xbc_ref, dt_ref = next(it), next(it), next(it)
    h0_ref = next(it) if has_h0 else None
    cw_ref, cb_ref, dtb_ref, alog_ref, dsk_ref, gn_ref = (next(it) for _ in range(6))
    if chain_state:
        next(it)
    y_ref = next(it)
    hf_ref = next(it) if want_state else None
    bc_s, xs_s, yf_s, yb_s, ld_s, da_s, h_s = (next(it) for _ in range(7))
    lhs_a, lhs_b, bwt_a, bwt_b, cd_a, cd_b = (next(it) for _ in range(6))

    nc = seq // SSD_CHUNK
    rb = SSD_CHUNK

    cstep = 512
    for r0 in range(0, seq, rb):
        for c0 in range(0, CONV_CH, cstep):
            cs = slice(c0, c0 + cstep)
            cur = xbc_ref[r0:r0 + rb, cs]
            rid = lax.broadcasted_iota(jnp.int32, (rb, cstep), 0)
            prev = pltpu.roll(cur, 1, 0)
            if r0 > 0:
                prev = jnp.where(rid == 0, xbc_ref[r0 - 1:r0, cs], prev)
            else:
                prev = jnp.where(rid == 0, 0.0, prev)
            nxt = pltpu.roll(cur, rb - 1, 0)
            if r0 + rb < seq:
                nxt = jnp.where(rid == rb - 1, xbc_ref[r0 + rb:r0 + rb + 1, cs], nxt)
            else:
                nxt = jnp.where(rid == rb - 1, 0.0, nxt)
            u = (prev * cw_ref[0, 0:1, cs] + cur * cw_ref[0, 1:2, cs] + nxt * cw_ref[0, 2:3, cs]
                 + cb_ref[0, :, cs])
            u = _silu(u)
            if c0 < D_INNER:
                xs_s[r0:r0 + rb, cs] = u.astype(BF16)
                yf_s[r0:r0 + rb, cs] = u * dsk_ref[0, :, cs]
            else:
                bc_s[r0:r0 + rb, c0 - D_INNER:c0 - D_INNER + cstep] = u

    a_row = -jnp.exp(alog_ref[0]) * LOG2E
    for r0 in range(0, seq, rb):
        v = dt_ref[r0:r0 + rb, :] + dtb_ref[0]
        dtv = jnp.maximum(v, 0.0) + jnp.log1p(jnp.exp(-jnp.abs(v)))
        ld_s[r0:r0 + rb, :] = jnp.log2(dtv)
        da_s[r0:r0 + rb, :] = dtv * a_row

    for d in range(2):
        if has_h0:
            for hp in range(SSD_HEADS // 2):
                blk = h0_ref[0, 0, d, hp * LANES:(hp + 1) * LANES, :]
                h_s[d, :, hp * LANES:(hp + 1) * LANES] = blk.T
        else:
            h_s[d] = jnp.zeros((SSD_STATE, D_INNER), F32)

    ri = lax.broadcasted_iota(jnp.int32, (rb, rb), 0)
    ci = lax.broadcasted_iota(jnp.int32, (rb, rb), 1)
    lower = ri >= ci
    upper = ri <= ci
    tri_f = jnp.where(lower, 1.0, 0.0).astype(BF16)
    tri_b = jnp.where(upper, 1.0, 0.0).astype(BF16)
    first_half = ci < SSD_HEAD_DIM

    def rows_of(step, d):
        c = step if d == 0 else nc - 1 - step
        return c * rb if isinstance(c, int) else pl.multiple_of(c * rb, rb)

    def build(step, stage):
        lhs_ref, bwt_ref, cd_ref = stage
        for d in range(2):
            r = rows_of(step, d)
            mask = lower if d == 0 else upper
            tri = tri_f if d == 0 else tri_b
            end = rb - 1 if d == 0 else 0
            hi, mid, lo = _split3(da_s[pl.ds(r, rb), :])
            acum = _dot(tri, hi) + _dot(tri, mid) + _dot(tri, lo)
            acum_t = acum.T
            rl_t = acum_t - ld_s[pl.ds(r, rb), :].T
            w_t = jnp.exp2(acum_t[:, end:end + 1] - rl_t)
            for g in range(SSD_GROUPS):
                b_g = bc_s[pl.ds(r, rb), g * SSD_STATE:(g + 1) * SSD_STATE]
                c_g = bc_s[pl.ds(r, rb), (SSD_GROUPS + g) * SSD_STATE:(SSD_GROUPS + g + 1) * SSD_STATE]
                cb = _dot_nt(c_g.astype(BF16), b_g.astype(BF16))
                b_t = b_g.T
                for e in range(g * 8, g * 8 + 8):
                    k = d * SSD_HEADS + e
                    colb = jnp.broadcast_to(acum[:, k:k + 1], (rb, rb))
                    m_e = jnp.exp2(jnp.where(mask, colb - rl_t[k:k + 1, :], NEG_BIG)) * cb
                    e_col = jnp.exp2(colb)
                    lhs_ref[d, e] = jnp.concatenate([m_e.astype(BF16), (c_g * e_col).astype(BF16)], axis=1)
                    bwt_ref[d, e] = (b_t * w_t[k:k + 1, :]).astype(BF16)
                    cd_ref[d, e] = e_col[end:end + 1, :]

    def apply(step, stage):
        lhs_ref, bwt_ref, cd_ref = stage
        for d in range(2):
            r = rows_of(step, d)
            for hp in range(SSD_HEADS // 2):
                ls = slice(hp * LANES, (hp + 1) * LANES)
                x_pair = xs_s[pl.ds(r, rb), ls]
                h_pair = h_s[d, :, ls]
                rhs = jnp.concatenate([x_pair, h_pair.astype(BF16)], axis=0)
                y0 = _dot(lhs_ref[d, 2 * hp], rhs)
                y1 = _dot(lhs_ref[d, 2 * hp + 1], rhs)
                s0 = _dot(bwt_ref[d, 2 * hp], x_pair)
                s1 = _dot(bwt_ref[d, 2 * hp + 1], x_pair)
                y_pair = jnp.where(first_half, y0, y1)
                if d == 0:
                    yf_s[pl.ds(r, rb), ls] = yf_s[pl.ds(r, rb), ls] + y_pair
                else:
                    yb_s[pl.ds(r, rb), ls] = y_pair
                cd = jnp.where(first_half[0:1, :], cd_ref[d, 2 * hp], cd_ref[d, 2 * hp + 1])
                h_s[d, :, ls] = h_pair * cd + jnp.where(first_half, s0, s1)

    stage_a = (lhs_a, bwt_a, cd_a)
    stage_b = (lhs_b, bwt_b, cd_b)
    build(0, stage_a)

    def body(j, carry):
        t = 2 * j
        build(t + 1, stage_b)
        apply(t, stage_a)
        build(t + 2, stage_a)
        apply(t + 1, stage_b)
        return carry

    n_loop = (nc - 2) // 2
    if n_loop > 0:
        lax.fori_loop(0, n_loop, body, 0)
    build(nc - 1, stage_b)
    apply(nc - 2, stage_a)
    apply(nc - 1, stage_b)

    for r0 in range(0, seq, rb):
        y = (yf_s[r0:r0 + rb, :] + yb_s[r0:r0 + rb, :]) * _silu(z_ref[r0:r0 + rb, :])
        y_ref[r0:r0 + rb, :] = _rms(y, gn_ref[0]).astype(y_ref.dtype)

    if want_state:
        for d in range(2):
            for hp in range(SSD_HEADS // 2):
                slot = 0 if chain_state else layer
                hf_ref[0, slot, d, hp * LANES:(hp + 1) * LANES, :] = h_s[d, :, hp * LANES:(hp + 1) * LANES].T
        if not chain_state:
            for other in range(DEPTH):
                if other != layer:
                    hf_ref[0, other] = jnp.zeros((2, D_INNER, SSD_STATE), F32)


def _ssd_branch(z, xbc, dt, h0, state_buf, pw, layer, batch, seq, want_state):
    has_h0 = h0 is not None
    chain_state = state_buf is not None
    row = lambda w: pl.BlockSpec((seq, w), lambda b: (b, 0))
    state_spec = pl.BlockSpec((1, 1, 2, D_INNER, SSD_STATE), lambda b: (b, layer, 0, 0, 0))
    in_specs = [row(D_INNER), row(CONV_CH), row(DT_PAD)]
    args = [z, xbc, dt]
    if has_h0:
        in_specs.append(state_spec)
        args.append(h0)
    consts = [pw["conv_ssd_w"], pw["conv_ssd_b"], pw["dt_bias"], pw["a_log"], pw["d_skip"], pw["g_ssd_norm"]]
    in_specs += [_layer_spec(c, layer) for c in consts]
    args += consts
    aliases = {}
    if chain_state:
        aliases = {len(args): 1}
        in_specs.append(pl.BlockSpec(memory_space=pl.ANY))
        args.append(state_buf)
    out_specs = [row(D_INNER)]
    out_shape = [jax.ShapeDtypeStruct((batch * seq, D_INNER), BF16)]
    if want_state:
        out_specs.append(state_spec if chain_state else
                         pl.BlockSpec((1, DEPTH, 2, D_INNER, SSD_STATE), lambda b: (b, 0, 0, 0, 0)))
        out_shape.append(jax.ShapeDtypeStruct((batch, DEPTH, 2, D_INNER, SSD_STATE), F32))
    res = pl.pallas_call(
        functools.partial(_ssd_kernel, seq=seq, has_h0=has_h0, want_state=want_state,
                          chain_state=chain_state, layer=layer),
        grid=(batch,),
        in_specs=in_specs,
        out_specs=out_specs,
        out_shape=out_shape,
        input_output_aliases=aliases,
        scratch_shapes=[
            pltpu.VMEM((seq, CONV_CH - D_INNER), F32),
            pltpu.VMEM((seq, D_INNER), BF16),
            pltpu.VMEM((seq, D_INNER), F32),
            pltpu.VMEM((seq, D_INNER), F32),
            pltpu.VMEM((seq, DT_PAD), F32),
            pltpu.VMEM((seq, DT_PAD), F32),
            pltpu.VMEM((2, SSD_STATE, D_INNER), F32),
        ] + 2 * [pltpu.VMEM((2, SSD_HEADS, SSD_CHUNK, 2 * SSD_CHUNK), BF16)]
          + 2 * [pltpu.VMEM((2, SSD_HEADS, SSD_STATE, SSD_CHUNK), BF16)]
          + 2 * [pltpu.VMEM((2, SSD_HEADS, 1, SSD_STATE), F32)],
        compiler_params=_params(("parallel",)),
        name="ssd_branch",
    )(*args)
    return (res[0], res[1]) if want_state else (res[0], None)


def _mla_kernel(*refs, seq, n_cache, chain_kv, layer):
    it = iter(refs)
    qd_ref, kvd_ref = next(it), next(it)
    latent = n_cache > 0
    qtab_ref = next(it)
    if latent:
        cckv_ref, ckpe_ref = next(it), next(it)
        kcos_ref, ksn_ref, ksp_ref = (next(it) for _ in range(3))
    gq_ref, gkv_ref, wq_ref, wkv_ref = (next(it) for _ in range(4))
    if chain_kv:
        next(it), next(it)
    o_ref = next(it)
    if not latent:
        ckv_out_ref, kpe_out_ref = next(it), next(it)
    q_s, k_s, v_s = (next(it) for _ in range(3))
    if latent:
        o_s, s_a, s_b, m_a, m_b, p_a, p_b = (next(it) for _ in range(7))

    rb = min(Q_BLOCK_ROWS, seq)
    n_pairs = MLA_HEADS // 2
    lane = lax.broadcasted_iota(jnp.int32, (rb, LANES), 1)
    first_half = lane < V_DIM
    ones_even = jnp.where(lane == V_DIM, 1.0, 0.0)
    ones_odd = jnp.where(lane == 0, 1.0, 0.0)

    def pair_cols(i):
        return slice(2 * i * HEAD_SLOT, 2 * (i + 1) * HEAD_SLOT)

    for r0 in range(0, seq, rb):
        qn = _rms(qd_ref[r0:r0 + rb, :], gq_ref[0]).astype(BF16)
        tab = qtab_ref[r0:r0 + rb, :] if latent else qtab_ref[...]
        for hp in range(n_pairs):
            qq = _dot(qn, wq_ref[0, :, pair_cols(hp)])
            q_s[2 * hp, r0:r0 + rb, :] = (qq[:, :HEAD_SLOT] * tab).astype(BF16)
            q_s[2 * hp + 1, r0:r0 + rb, :] = (qq[:, HEAD_SLOT:] * tab).astype(BF16)

    def put_keys(row0, ckv_n, kpe_tile):
        kpe_slot = pltpu.roll(kpe_tile, QK_NOPE, 1) + pltpu.roll(kpe_tile, QK_NOPE + ROPE_DIM, 1)
        cb16 = ckv_n.astype(BF16)
        for hp in range(n_pairs):
            kv = _dot(cb16, wkv_ref[0, :, pair_cols(hp)])
            kv_e, kv_o = kv[:, :HEAD_SLOT], kv[:, HEAD_SLOT:]
            k_s[2 * hp, row0:row0 + rb, :] = (jnp.where(first_half, kv_e, 0.0) + kpe_slot).astype(BF16)
            k_s[2 * hp + 1, row0:row0 + rb, :] = (jnp.where(first_half, kv_o, 0.0) + kpe_slot).astype(BF16)
            v_e = pltpu.roll(kv_e, V_DIM, 1)
            if latent:
                v_s[2 * hp, row0:row0 + rb, :] = (jnp.where(first_half, v_e, 0.0) + ones_even).astype(BF16)
                v_s[2 * hp + 1, row0:row0 + rb, :] = (jnp.where(first_half, 0.0, kv_o) + ones_odd).astype(BF16)
            else:
                v_s[hp, row0:row0 + rb, :] = jnp.where(first_half, v_e, kv_o).astype(BF16)

    for r0 in range(0, seq, rb):
        ckv_n = _rms(kvd_ref[r0:r0 + rb, 0:KV_LORA], gkv_ref[0])
        kpe = jnp.where(lane < ROPE_DIM, kvd_ref[r0:r0 + rb, KV_LORA:KV_PAD], 0.0)
        if latent:
            kpe = (kpe * kcos_ref[r0:r0 + rb, :]
                   + pltpu.roll(kpe, LANES - 8, 1) * ksn_ref[r0:r0 + rb, :]
                   + pltpu.roll(kpe, 8, 1) * ksp_ref[r0:r0 + rb, :])
        else:
            slot = 0 if chain_kv else layer
            ckv_out_ref[0, slot, r0:r0 + rb, :] = ckv_n
            kpe_out_ref[0, slot, r0:r0 + rb, :] = kpe[:, 0:ROPE_DIM]
            if not chain_kv:
                for other in range(DEPTH):
                    if other != layer:
                        ckv_out_ref[0, other, r0:r0 + rb, :] = jnp.zeros((rb, KV_LORA), F32)
                        kpe_out_ref[0, other, r0:r0 + rb, :] = jnp.zeros((rb, ROPE_DIM), F32)
        put_keys(r0, ckv_n, kpe)
    for r0 in range(0, n_cache, rb):
        put_keys(seq + r0, cckv_ref[0, 0, r0:r0 + rb, :], ckpe_ref[0, 0, r0:r0 + rb, :])

    def normalise(o0, o1):
        inv0 = 1.0 / o0[:, V_DIM:V_DIM + 1]
        inv1 = 1.0 / o1[:, 0:1]
        return jnp.where(first_half, o0 * inv0, o1 * inv1).astype(BF16)

    if not latent:
        for hp in range(n_pairs):
            outs = []
            for j in range(2):
                h = 2 * hp + j
                s = _dot_nt(q_s[h], k_s[h])
                p = jnp.exp2(s - jnp.max(s, axis=-1, keepdims=True))
                l = jnp.sum(p, axis=-1, keepdims=True)
                outs.append(_dot(p.astype(BF16), v_s[hp]) * (1.0 / l))
            o_ref[:, hp * LANES:(hp + 1) * LANES] = jnp.where(first_half, outs[0], outs[1]).astype(BF16)
        return

    nqb = seq // rb
    n_units = n_pairs * nqb

    def unit(u):
        if isinstance(u, int):
            return u // nqb, (u % nqb) * rb
        return u // nqb, pl.multiple_of((u % nqb) * rb, rb)

    def scores(u, s_ref, m_ref):
        hp, r = unit(u)
        for j in range(2):
            s = _dot_nt(q_s[2 * hp + j, pl.ds(r, rb), :], k_s[2 * hp + j])
            s_ref[j] = s
            m_ref[j] = jnp.max(s, axis=-1, keepdims=True)

    def probs(s_ref, m_ref, p_ref):
        for j in range(2):
            p_ref[j] = jnp.exp2(s_ref[j] - m_ref[j]).astype(BF16)

    def values(u, p_ref):
        hp, r = unit(u)
        o_s[hp, pl.ds(r, rb), :] = normalise(_dot(p_ref[0], v_s[2 * hp]), _dot(p_ref[1], v_s[2 * hp + 1]))

    scores(0, s_a, m_a)
    scores(1, s_b, m_b)
    probs(s_a, m_a, p_a)

    def body(j, carry):
        u = 2 * j
        scores(u, s_a, m_a)
        probs(s_b, m_b, p_b)
        values(u - 2, p_a)
        scores(u + 1, s_b, m_b)
        probs(s_a, m_a, p_a)
        values(u - 1, p_b)
        return carry

    lax.fori_loop(1, n_units // 2, body, 0)
    probs(s_b, m_b, p_b)
    values(n_units - 2, p_a)
    values(n_units - 1, p_b)

    for hp in range(n_pairs):
        o_ref[:, hp * LANES:(hp + 1) * LANES] = o_s[hp]


def _mla_branch(qd, kvd, cache, rope_tabs, kv_bufs, pw, layer, batch, seq):
    latent = cache is not None
    chain_kv = kv_bufs is not None
    n_cache = cache[0].shape[2] if latent else 0
    row = lambda w: pl.BlockSpec((seq, w), lambda b: (b, 0))
    in_specs = [row(Q_LORA), row(KV_PAD), _const_spec(rope_tabs[0].shape)]
    args = [qd, kvd, rope_tabs[0]]
    if latent:
        in_specs += [pl.BlockSpec((1, 1, n_cache, KV_LORA), lambda b: (b, layer, 0, 0)),
                     pl.BlockSpec((1, 1, n_cache, LANES), lambda b: (b, layer, 0, 0))]
        args += list(cache)
        in_specs += [_const_spec((seq, LANES))] * 3
        args += list(rope_tabs[1:])
    consts = [pw["g_q_norm"], pw["g_kv_norm"], pw["w_q_pad"], pw["w_kv_up"]]
    in_specs += [_layer_spec(c, layer) for c in consts]
    args += consts
    aliases = {}
    if chain_kv:
        aliases = {len(args): 1, len(args) + 1: 2}
        in_specs += [pl.BlockSpec(memory_space=pl.ANY)] * 2
        args += list(kv_bufs)
    out_specs = [row(D_MODEL)]
    out_shape = [jax.ShapeDtypeStruct((batch * seq, D_MODEL), BF16)]
    if not latent:
        n_slot, first = (1, layer) if chain_kv else (DEPTH, 0)
        out_specs += [pl.BlockSpec((1, n_slot, seq, KV_LORA), lambda b: (b, first, 0, 0)),
                      pl.BlockSpec((1, n_slot, seq, ROPE_DIM), lambda b: (b, first, 0, 0))]
        out_shape += [jax.ShapeDtypeStruct((batch, DEPTH, seq, KV_LORA), F32),
                      jax.ShapeDtypeStruct((batch, DEPTH, seq, ROPE_DIM), F32)]
    n_keys = seq + n_cache
    scratch = [
        pltpu.VMEM((MLA_HEADS, seq, HEAD_SLOT), BF16),
        pltpu.VMEM((MLA_HEADS, n_keys, HEAD_SLOT), BF16),
        pltpu.VMEM((MLA_HEADS if latent else MLA_HEADS // 2, n_keys, LANES), BF16),
    ]
    if latent:
        tile = (2, min(Q_BLOCK_ROWS, seq), n_keys)
        stat = (2, min(Q_BLOCK_ROWS, seq), 1)
        scratch += [pltpu.VMEM((MLA_HEADS // 2, seq, LANES), BF16),
                    pltpu.VMEM(tile, F32), pltpu.VMEM(tile, F32),
                    pltpu.VMEM(stat, F32), pltpu.VMEM(stat, F32),
                    pltpu.VMEM(tile, BF16), pltpu.VMEM(tile, BF16)]
    return pl.pallas_call(
        functools.partial(_mla_kernel, seq=seq, n_cache=n_cache, chain_kv=chain_kv, layer=layer),
        grid=(batch,),
        in_specs=in_specs,
        out_specs=out_specs,
        out_shape=out_shape,
        input_output_aliases=aliases,
        scratch_shapes=scratch,
        compiler_params=_params(("parallel",)),
        name="mla_branch",
    )(*args)


def _fft_kernel(f_ref, cs_ref, dl_ref, o_ref, *, seq, n_seq):
    scale = 1.0 / math.sqrt(seq * FFT_GROUP_W)
    for s in range(n_seq):
        rs = slice(s * seq, (s + 1) * seq)
        for g in range(FFT_GROUPS):
            gs = slice(g * FFT_GROUP_W, (g + 1) * FFT_GROUP_W)
            t = _dot(f_ref[rs, gs], cs_ref[...]).astype(BF16)
            stack = jnp.concatenate([t[:, :FFT_GROUP_W], t[:, FFT_GROUP_W:]], axis=0)
            o_ref[rs, gs] = (_dot(dl_ref[...], stack) * scale).astype(o_ref.dtype)


def _fft_branch(f_in, cs, dl, batch, seq):
    n_seq = max(FFT_ROWS // seq, 1)
    row = pl.BlockSpec((n_seq * seq, D_MODEL), lambda b: (b, 0))
    return pl.pallas_call(
        functools.partial(_fft_kernel, seq=seq, n_seq=n_seq),
        grid=(batch // n_seq,),
        in_specs=[row, _const_spec(cs.shape), _const_spec(dl.shape)],
        out_specs=row,
        out_shape=jax.ShapeDtypeStruct((batch * seq, D_MODEL), BF16),
        compiler_params=_params(("parallel",)),
        name="fft_branch",
    )(f_in, cs, dl)


def _dft_tables(seq):
    def cos_sin(n):
        idx = np.arange(n, dtype=np.int64)
        ang = 2.0 * np.pi * ((idx[:, None] * idx[None, :]) % n) / n
        return np.cos(ang), np.sin(ang)

    cc, sc = cos_sin(FFT_GROUP_W)
    cl, sl = cos_sin(seq)
    cs = np.concatenate([cc, sc], axis=1).astype(np.float32)
    dl = np.concatenate([cl, -sl], axis=1).astype(np.float32)
    return jnp.asarray(cs).astype(BF16), jnp.asarray(dl).astype(BF16)


def _merge_kernel(y_ref, o_ref, f_ref, gate_ref, x_ref, g1_ref, gp_ref,
                  ws_ref, wm_ref, wf_ref, wo_ref, out_ref, *, mod_row):
    row = mod_row(pl.program_id(0))
    mix = (gate_ref[:, 0:D_MODEL] * _dot(y_ref[...], ws_ref[0])
           + gate_ref[:, D_MODEL:2 * D_MODEL] * _dot(o_ref[...], wm_ref[0])
           + gate_ref[:, 2 * D_MODEL:3 * D_MODEL] * _dot(f_ref[...], wf_ref[0]))
    out = _dot(mix.astype(BF16), wo_ref[0])
    out_ref[...] = x_ref[...] + _mod_row(g1_ref, row) * _rms(out, gp_ref[0])


def _merge(y, o, f, gates, x, mods, pw, layer, seq, latent):
    t = x.shape[0]
    row = lambda w: pl.BlockSpec((MERGE_ROWS, w), lambda i: (i, 0))
    consts = [pw["g_post_mix"], pw["w_ssd_out"], pw["w_mla_out"], pw["w_fft_out"], pw["w_o"]]
    return pl.pallas_call(
        functools.partial(_merge_kernel, mod_row=_row_of_tile(max(seq // MERGE_ROWS, 1), latent)),
        grid=(t // MERGE_ROWS,),
        in_specs=[row(D_MODEL), row(D_MODEL), row(D_MODEL), row(N_BRANCH * D_MODEL), row(D_MODEL),
                  _mod_spec(layer, 2)] + [_layer_spec(c, layer) for c in consts],
        out_specs=row(D_MODEL),
        out_shape=jax.ShapeDtypeStruct((t, D_MODEL), F32),
        compiler_params=_params(("parallel",)),
        name="merge",
    )(y, o, f, gates, x, mods, *consts)


def _ffn_kernel(x_ref, sh_ref, sc_ref, g2_ref, gpre_ref, gpost_ref,
                wu_ref, cw_ref, cb_ref, wd_ref,
                out_ref, h_s, acc_s, u_a, u_b, *, seq, mod_row):
    row = mod_row(pl.program_id(0))
    h_s[...] = (_rms(x_ref[...], gpre_ref[0]) * (1.0 + _mod_row(sc_ref, row))
                + _mod_row(sh_ref, row)).astype(BF16)

    row8 = lax.broadcasted_iota(jnp.int32, (SUBLANES, FFN_BLOCK), 0)

    def neighbours(u):
        prev = pltpu.roll(u, 1, 0)
        nxt = pltpu.roll(u, FFN_ROWS - 1, 0)
        pp, nn = [], []
        for s0 in range(0, FFN_ROWS, seq):
            pp += [jnp.where(row8 == 0, 0.0, prev[s0:s0 + SUBLANES]), prev[s0 + SUBLANES:s0 + seq]]
            nn += [nxt[s0:s0 + seq - SUBLANES],
                   jnp.where(row8 == SUBLANES - 1, 0.0, nxt[s0 + seq - SUBLANES:s0 + seq])]
        return jnp.concatenate(pp, axis=0), jnp.concatenate(nn, axis=0)

    def cols(k, gate_half):
        c0 = (D_FF if gate_half else 0) + k * FFN_BLOCK
        if isinstance(k, int):
            return slice(c0, c0 + FFN_BLOCK)
        return pl.ds(pl.multiple_of(c0, FFN_BLOCK), FFN_BLOCK)

    def rows(k):
        if isinstance(k, int):
            return slice(k * FFN_BLOCK, (k + 1) * FFN_BLOCK)
        return pl.ds(pl.multiple_of(k * FFN_BLOCK, FFN_BLOCK), FFN_BLOCK)

    def up(k, u_ref):
        h = h_s[...]
        u_ref[0] = _dot(h, wu_ref[0, :, cols(k, False)])
        u_ref[1] = _dot(h, wu_ref[0, :, cols(k, True)])

    def conv(u, cs):
        prev, nxt = neighbours(u)
        return (prev * cw_ref[0, 0:1, cs] + u * cw_ref[0, 1:2, cs] + nxt * cw_ref[0, 2:3, cs]
                + cb_ref[0, :, cs])

    def down(k, u_ref, first=False):
        a = conv(u_ref[0], cols(k, False))
        g = conv(u_ref[1], cols(k, True))
        d = _dot((_silu(a) * g).astype(BF16), wd_ref[0, rows(k), :])
        if first:
            acc_s[...] = d
        else:
            acc_s[...] += d

    up(0, u_a)
    up(1, u_b)
    down(0, u_a, first=True)

    def body(i, carry):
        up(2 * i, u_a)
        down(2 * i - 1, u_b)
        up(2 * i + 1, u_b)
        down(2 * i, u_a)
        return carry

    n_pairs = (FFN_NBLK - 1) // 2
    lax.fori_loop(1, n_pairs, body, 0)
    last = 2 * n_pairs
    up(last, u_a)
    down(last - 1, u_b)
    down(last, u_a)

    out_ref[...] = x_ref[...] + _mod_row(g2_ref, row) * _rms(acc_s[...], gpost_ref[0])


def _ffn(x, mods, pw, layer, seq, latent):
    t = x.shape[0]
    row = pl.BlockSpec((FFN_ROWS, D_MODEL), lambda i: (i, 0))
    consts = [pw["g_pre_ffn"], pw["g_post_ffn"], pw["w_up"], pw["conv_ffn_w"], pw["conv_ffn_b"], pw["w_down"]]
    stage = pltpu.VMEM((2, FFN_ROWS, FFN_BLOCK), F32)
    return pl.pallas_call(
        functools.partial(_ffn_kernel, seq=seq, mod_row=_row_of_tile(max(seq // FFN_ROWS, 1), latent)),
        grid=(t // FFN_ROWS,),
        in_specs=[row, _mod_spec(layer, 3), _mod_spec(layer, 4), _mod_spec(layer, 5)]
        + [_layer_spec(c, layer) for c in consts],
        out_specs=row,
        out_shape=jax.ShapeDtypeStruct((t, D_MODEL), F32),
        scratch_shapes=[pltpu.VMEM((FFN_ROWS, D_MODEL), BF16), pltpu.VMEM((FFN_ROWS, D_MODEL), F32),
                        stage, stage],
        compiler_params=_params(("parallel",)),
        name="conv_mlp",
    )(x, mods, mods, mods, *consts)


def _rope_partner():
    idx = np.arange(ROPE_DIM)
    quarter = ROPE_DIM // 4
    return np.where(idx % (2 * quarter) < quarter, idx + quarter, idx - quarter)


def _pad_q_up(w_q_up):
    w = w_q_up.reshape(DEPTH, Q_LORA, MLA_HEADS, QK_NOPE + ROPE_DIM)
    pe = w[..., QK_NOPE:]
    w = jnp.concatenate([w, pe[..., _rope_partner()]], axis=-1)
    return w.reshape(DEPTH, Q_LORA, MLA_HEADS * HEAD_SLOT).astype(BF16)


def _rope_tables(seq):
    rows = seq // GRID_W
    row = jnp.repeat(jnp.arange(rows), GRID_W)
    col = jnp.tile(jnp.arange(GRID_W), rows)
    quarter = ROPE_DIM // 4
    inv = ROPE_THETA ** (-jnp.arange(quarter, dtype=F32) / quarter)
    ang_row = row.astype(F32)[:, None] * inv
    ang_col = col.astype(F32)[:, None] * inv
    cos = jnp.concatenate([jnp.cos(ang_row)] * 2 + [jnp.cos(ang_col)] * 2, axis=1)
    sin = jnp.concatenate([jnp.sin(ang_row)] * 2 + [jnp.sin(ang_col)] * 2, axis=1)
    lo = (jnp.arange(ROPE_DIM) % (2 * quarter)) < quarter
    sin_signed = jnp.where(lo, -sin, sin)
    scale = ATTN_SCALE * LOG2E
    q_tab = jnp.concatenate([jnp.full((seq, QK_NOPE), scale, F32), scale * cos, scale * sin_signed], axis=1)
    pad = lambda t, fill: jnp.concatenate([t, jnp.full((seq, LANES - ROPE_DIM), fill, F32)], axis=1)
    return (q_tab, pad(cos, 1.0), pad(jnp.where(lo, -sin, 0.0), 0.0), pad(jnp.where(lo, 0.0, sin), 0.0))


def _ctx_q_table():
    scale = ATTN_SCALE * LOG2E
    return jnp.concatenate([jnp.full((1, QK_NOPE + ROPE_DIM), scale, F32),
                            jnp.zeros((1, HEAD_SLOT - QK_NOPE - ROPE_DIM), F32)], axis=1)


def _trunk_pass(x, mods, pw, layer, ctx, outs, tabs, batch, seq):
    latent = ctx is not None
    z, xbc, dt, qd, kvd, f_in, gates = _pre_mix(x, mods, pw, layer, seq, latent)
    y_ssd, ssm = _ssd_branch(z, xbc, dt, ctx[2] if latent else None, outs[2] if outs else None,
                             pw, layer, batch, seq, want_state=not latent)
    mla = _mla_branch(qd, kvd, (ctx[0], ctx[1]) if latent else None, tabs["rope"],
                      (outs[0], outs[1]) if outs else None, pw, layer, batch, seq)
    f = _fft_branch(f_in, tabs["cs"], tabs["dl"], batch, seq)
    x1 = _merge(y_ssd, mla[0], f, gates, x, mods, pw, layer, seq, latent)
    x2 = _ffn(x1, mods, pw, layer, seq, latent)
    if latent:
        return x2, None
    return x2, (mla[1], mla[2], ssm)


def kernel(x_prompt, x_sample, cache_ckv, cache_kpe, state_ssm, c, c_ctx, w_ada, b_ada, g_pre_mix, g_post_mix, g_pre_ffn, g_post_ffn, w_in, w_gate, b_gate, w_o, conv_ssd_w, conv_ssd_b, dt_bias, a_log, d_skip, g_ssd_norm, w_ssd_out, g_q_norm, w_q_up, g_kv_norm, w_kv_up, w_mla_out, w_fft_out, w_up, conv_ffn_w, conv_ffn_b, w_down):
    batch, seq, _ = x_prompt.shape
    dec_batch, dec_seq, _ = x_sample.shape

    cvec = jnp.concatenate([c_ctx[None, :], c, jnp.zeros((SUBLANES - 1 - dec_batch, D_MODEL), F32)], axis=0)
    mods = _ada_params(cvec, w_ada, b_ada)

    vec = lambda v: v[:, None, :]
    pad_lane = lambda v: jnp.pad(v, ((0, 0), (0, 0), (0, LANES - v.shape[-1])))
    pw = {
        "g_pre_mix": vec(g_pre_mix), "g_post_mix": vec(g_post_mix),
        "g_pre_ffn": vec(g_pre_ffn), "g_post_ffn": vec(g_post_ffn),
        "w_in": jnp.swapaxes(w_in, 1, 2).astype(BF16),
        "w_gate": w_gate.astype(BF16), "b_gate": vec(b_gate),
        "w_o": w_o.astype(BF16), "w_ssd_out": w_ssd_out.astype(BF16),
        "w_mla_out": w_mla_out.astype(BF16), "w_fft_out": w_fft_out.astype(BF16),
        "conv_ssd_w": conv_ssd_w, "conv_ssd_b": vec(conv_ssd_b),
        "dt_bias": pad_lane(dt_bias.reshape(DEPTH, 1, 2 * SSD_HEADS)),
        "a_log": pad_lane(a_log.reshape(DEPTH, 1, 2 * SSD_HEADS)),
        "d_skip": vec(jnp.repeat(d_skip, SSD_HEAD_DIM, axis=1)),
        "g_ssd_norm": vec(g_ssd_norm),
        "g_q_norm": vec(g_q_norm), "g_kv_norm": vec(g_kv_norm),
        "w_q_pad": _pad_q_up(w_q_up), "w_kv_up": w_kv_up.astype(BF16),
        "w_up": w_up.astype(BF16), "conv_ffn_w": conv_ffn_w, "conv_ffn_b": vec(conv_ffn_b),
        "w_down": w_down.astype(BF16),
    }

    cs, dl_ctx = _dft_tables(seq)
    _, dl_lat = _dft_tables(dec_seq)
    tabs_ctx = {"rope": (_ctx_q_table(),), "cs": cs, "dl": dl_ctx}
    tabs_lat = {"rope": _rope_tables(dec_seq), "cs": cs, "dl": dl_lat}
    cache_kpe_p = jnp.pad(cache_kpe, ((0, 0), (0, 0), (0, 0), (0, LANES - ROPE_DIM)))
    h0_all = state_ssm.reshape(dec_batch, DEPTH, 2, D_INNER, SSD_STATE)
    ctx = (cache_ckv, cache_kpe_p, h0_all)

    y_p = x_prompt.reshape(batch * seq, D_MODEL)
    y_s = x_sample.reshape(dec_batch * dec_seq, D_MODEL)
    outs = None
    for i in range(DEPTH):
        y_p, outs = _trunk_pass(y_p, mods, pw, i, None, outs, tabs_ctx, batch, seq)
        y_s, _ = _trunk_pass(y_s, mods, pw, i, ctx, None, tabs_lat, dec_batch, dec_seq)
    new_ckv, new_kpe, new_ssm = outs
    return (y_p.reshape(batch, seq, D_MODEL), y_s.reshape(dec_batch, dec_seq, D_MODEL),
            new_ckv, new_kpe, new_ssm.reshape(batch, DEPTH, 2, SSD_HEADS, SSD_HEAD_DIM, SSD_STATE))
```
